```python
import math
import jax, jax.numpy as jnp
from jax import lax
import numpy as np

D_MODEL = 1024
BATCH = 8
SEQ = 2048
DEPTH = 2

GRID_W = 64
NA_HEADS = 8
NA_HEAD_DIM = 64
NA_WIDTH = NA_HEADS * NA_HEAD_DIM
NA_KH = 8
NA_KW = 16
NA_QB = 16
NA_KB = NA_QB + NA_KW
NA_NCB = GRID_W // NA_QB
HY_WIDTH = D_MODEL - NA_WIDTH
D_MIX = NA_WIDTH + HY_WIDTH
SHORT_CONV = 3
FILTER_EMB = 33
FILTER_ORDER = 64
DECAY_TARGET = 1e-2
FAST_DECAY_PCT = 0.3
SLOW_DECAY_PCT = 1.5
MAX_DECAY = math.log(DECAY_TARGET) / FAST_DECAY_PCT
MIN_DECAY = math.log(DECAY_TARGET) / SLOW_DECAY_PCT
D_FF = 2816
PLE_DIM = 256
EPS = 1e-6

kernel_name = 'hybrid_na_hyena_macaron_block'


def _rms(x):
    xf = x.astype(jnp.float32)
    return (xf * lax.rsqrt(jnp.mean(xf * xf, axis=-1, keepdims=True) + EPS)).astype(x.dtype)


def rmsnorm(x, g):
    return _rms(x) * g


def swiglu(h, w_gate, w_up, w_down):
    return (jax.nn.silu(h @ w_gate) * (h @ w_up)) @ w_down


def neighbourhood_attention(q, k, v, rpb):
    B, L, H, dh = q.shape
    rows = L // GRID_W
    kh = min(NA_KH, rows)
    nk = kh * NA_KB
    r = jnp.arange(rows)
    row_start = jnp.clip(r - kh // 2, 0, rows - kh)
    key_rows = row_start[:, None] + jnp.arange(kh)[None, :]
    qcol = jnp.arange(NA_NCB)[:, None] * NA_QB + jnp.arange(NA_QB)[None, :]
    band_c0 = jnp.clip(jnp.arange(NA_NCB) * NA_QB - NA_KW // 2, 0, GRID_W - NA_KB)
    key_cols = band_c0[:, None] + jnp.arange(NA_KB)[None, :]
    key_idx = (key_rows[:, None, :, None] * GRID_W + key_cols[None, :, None, :]).reshape(rows, NA_NCB, nk)
    kg = k[:, key_idx]
    vg = v[:, key_idx]
    qb = q.reshape(B, rows, NA_NCB, NA_QB, H, dh)
    s = jnp.einsum('brjqhd,brjkhd->bhrjqk', qb, kg).astype(jnp.float32) * (dh ** -0.5)
    krow = jnp.broadcast_to(key_rows[:, :, None], (rows, kh, NA_KB)).reshape(rows, nk)
    kcol = jnp.broadcast_to(key_cols[:, None, :], (NA_NCB, kh, NA_KB)).reshape(NA_NCB, nk)
    dr = krow - r[:, None]
    dc = kcol[:, None, :] - qcol[:, :, None]
    qcs = jnp.clip(qcol - NA_KW // 2, 0, GRID_W - NA_KW)
    mask = (kcol[:, None, :] >= qcs[..., None]) & (kcol[:, None, :] < qcs[..., None] + NA_KW)
    dr_idx = (dr + NA_KH - 1)[:, None, None, :]
    dc_idx = jnp.clip(dc + NA_KW - 1, 0, 2 * NA_KW - 2)[None]
    bias = rpb[:, dr_idx, dc_idx].astype(jnp.float32)
    s = jnp.where(mask[None, None, None], s + bias[None], -jnp.inf)
    prob = jax.nn.softmax(s, axis=-1).astype(vg.dtype)
    o = jnp.einsum('bhrjqk,brjkhd->brjqhd', prob, vg)
    return o.reshape(B, L, H * dh)


def short_conv(u, w, b):
    up = jnp.pad(u, ((0, 0), (1, 1), (0, 0)))
    return up[:, :-2] * w[0] + up[:, 1:-1] * w[1] + up[:, 2:] * w[2] + b


def implicit_filter(L, w_f1, b_f1, w_f2, b_f2, w_f3, b_f3, w_f4, freq):
    bands = (FILTER_EMB - 1) // 2
    t = jnp.linspace(0.0, 1.0, L, dtype=jnp.float32)[:, None]
    w = 2.0 * math.pi * jnp.arange(L, dtype=jnp.float32)[:, None] / L
    f = jnp.linspace(1e-4, bands - 1, bands, dtype=jnp.float32)[None, :]
    z = jnp.concatenate([t, jnp.cos(f * w), -jnp.sin(f * w)], axis=-1)
    h = jnp.sin(freq * (z @ w_f1 + b_f1))
    h = jnp.sin(freq * (h @ w_f2 + b_f2))
    h = jnp.sin(freq * (h @ w_f3 + b_f3))
    h = (h @ w_f4).astype(jnp.float32)
    deltas = jnp.abs(jnp.linspace(MIN_DECAY, MAX_DECAY, HY_WIDTH, dtype=jnp.float32))
    decay = jnp.exp(-t * deltas[None, :])
    h = h.reshape(L, 2, HY_WIDTH) * decay[:, None, :]
    h_fwd, h_bwd = h[:, 0], h[:, 1]
    k = jnp.concatenate([h_fwd[:1] + h_bwd[:1], h_fwd[1:], jnp.zeros((1, HY_WIDTH), jnp.float32),
                         h_bwd[1:][::-1]], axis=0)
    return k / jnp.sum(jnp.abs(k), axis=0, keepdims=True)


def long_conv(u, k, bias):
    L = u.shape[1]
    n = 2 * L
    u_f = jnp.fft.rfft(u.astype(jnp.float32), n=n, axis=1)
    k_f = jnp.fft.rfft(k, n=n, axis=0)
    y = jnp.fft.irfft(u_f * k_f[None], n=n, axis=1)[:, :L]
    return y.astype(u.dtype) + u * bias


def setup_inputs(seed: int = 0) -> dict:
    key = jax.random.key(seed)
    ks = iter(jax.random.split(key, 40))

    def nrm(shape, scale):
        return jax.random.normal(next(ks), shape, jnp.float32) * scale

    def gain(shape):
        return 1.0 + nrm(shape, 0.02)

    D = DEPTH
    return {
        'x': nrm((BATCH, SEQ, D_MODEL), 1.0),
        'p': nrm((DEPTH, BATCH, SEQ, PLE_DIM), 1.0),
        'g_ffa': gain((D, D_MODEL)),
        'w_ffa_gate': nrm((D, D_MODEL, D_FF), D_MODEL ** -0.5),
        'w_ffa_up': nrm((D, D_MODEL, D_FF), D_MODEL ** -0.5),
        'w_ffa_down': nrm((D, D_FF, D_MODEL), D_FF ** -0.5),
        'g_mix': gain((D, D_MODEL)),
        'w_in': nrm((D, D_MODEL, 3 * NA_WIDTH + 3 * HY_WIDTH), D_MODEL ** -0.5),
        'na_rpb': nrm((D, NA_HEADS, 2 * NA_KH - 1, 2 * NA_KW - 1), 0.02),
        'w_sc': nrm((D, SHORT_CONV, 3 * HY_WIDTH), SHORT_CONV ** -0.5),
        'b_sc': nrm((D, 3 * HY_WIDTH), 0.02),
        'w_f1': nrm((D, FILTER_EMB, FILTER_ORDER), FILTER_EMB ** -0.5),
        'b_f1': nrm((D, FILTER_ORDER), 0.02),
        'w_f2': nrm((D, FILTER_ORDER, FILTER_ORDER), FILTER_ORDER ** -0.5),
        'b_f2': nrm((D, FILTER_ORDER), 0.02),
        'w_f3': nrm((D, FILTER_ORDER, FILTER_ORDER), FILTER_ORDER ** -0.5),
        'b_f3': nrm((D, FILTER_ORDER), 0.02),
        'w_f4': nrm((D, FILTER_ORDER, 2 * HY_WIDTH), FILTER_ORDER ** -0.5),
        'filt_freq': gain((D, FILTER_ORDER)),
        'hy_bias': nrm((D, HY_WIDTH), 1.0),
        'g_out': gain((D, D_MIX)),
        'w_out': nrm((D, D_MIX, D_MODEL), D_MIX ** -0.5),
        'g_ffb': gain((D, D_MODEL)),
        'w_ffb_gate': nrm((D, D_MODEL, D_FF), D_MODEL ** -0.5),
        'w_ffb_up': nrm((D, D_MODEL, D_FF), D_MODEL ** -0.5),
        'w_ffb_down': nrm((D, D_FF, D_MODEL), D_FF ** -0.5),
        'g_ple': gain((D, D_MODEL)),
        'w_ple_gate': nrm((D, D_MODEL, D_MODEL), D_MODEL ** -0.5),
        'w_ple_proj': nrm((D, PLE_DIM, D_MODEL), PLE_DIM ** -0.5),
        'g_final': gain((D_MODEL,)),
    }


def reference(x, p, g_ffa, w_ffa_gate, w_ffa_up, w_ffa_down, g_mix, w_in, na_rpb, w_sc, b_sc,
              w_f1, b_f1, w_f2, b_f2, w_f3, b_f3, w_f4, filt_freq, hy_bias, g_out, w_out,
              g_ffb, w_ffb_gate, w_ffb_up, w_ffb_down, g_ple, w_ple_gate, w_ple_proj, g_final):
    B, L, _ = x.shape
    for i in range(DEPTH):
        x = x + 0.5 * swiglu(rmsnorm(x, g_ffa[i]), w_ffa_gate[i], w_ffa_up[i], w_ffa_down[i])
        u = rmsnorm(x, g_mix[i]) @ w_in[i]
        q, k, v, hy = jnp.split(u, [NA_WIDTH, 2 * NA_WIDTH, 3 * NA_WIDTH], axis=-1)
        hs = (B, L, NA_HEADS, NA_HEAD_DIM)
        y_na = neighbourhood_attention(q.reshape(hs), k.reshape(hs), v.reshape(hs), na_rpb[i])
        hy = short_conv(hy, w_sc[i], b_sc[i])
        x0, x1, hv = jnp.split(hy, 3, axis=-1)
        filt = implicit_filter(L, w_f1[i], b_f1[i], w_f2[i], b_f2[i], w_f3[i], b_f3[i], w_f4[i], filt_freq[i])
        y_hy = x0 * long_conv(hv * x1, filt, hy_bias[i])
        y = jnp.concatenate([_rms(y_na), _rms(y_hy)], axis=-1) * g_out[i]
        x = x + y @ w_out[i]
        x = x + 0.5 * swiglu(rmsnorm(x, g_ffb[i]), w_ffb_gate[i], w_ffb_up[i], w_ffb_down[i])
        x = x + jax.nn.sigmoid(rmsnorm(x, g_ple[i]) @ w_ple_gate[i]) * (p[i] @ w_ple_proj[i])
    return rmsnorm(x, g_final)
```

```python
import functools
import math

import numpy as np
import jax
import jax.numpy as jnp
from jax import lax
from jax.experimental import pallas as pl
from jax.experimental.pallas import tpu as pltpu

F32 = jnp.float32
BF16 = jnp.bfloat16

D_MODEL = 1024
GRID_W = 64
NA_HEADS = 8
NA_HEAD_DIM = 64
NA_WIDTH = NA_HEADS * NA_HEAD_DIM
NA_KH = 8
NA_KW = 16
HY_WIDTH = D_MODEL - NA_WIDTH
FILTER_EMB = 33
DECAY_TARGET = 1e-2
FAST_DECAY_PCT = 0.3
SLOW_DECAY_PCT = 1.5
MAX_DECAY = math.log(DECAY_TARGET) / FAST_DECAY_PCT
MIN_DECAY = math.log(DECAY_TARGET) / SLOW_DECAY_PCT
D_FF = 2816
PLE_DIM = 256
EPS = 1e-6

LANES = 128
ROW_TILE = 256
FF_CHUNKS = ((0, 1024), (1024, 2048), (2048, D_FF))
NA_QROWS = 4
NA_KROWS = 12
HY_CTILE = 256
HY_RCHUNK = 512
VMEM_LIMIT = 56 * 1024 * 1024


def _rms(x):
    return x * lax.rsqrt(jnp.mean(x * x, axis=-1, keepdims=True) + EPS)


def _dot(a, b):
    return jnp.dot(a, b, preferred_element_type=F32)


def _swiglu_residual(x, g, wg_ref, wu_ref, wd_ref):
    h = (_rms(x) * g).astype(BF16)
    acc = None
    for lo, hi in FF_CHUNKS:
        gate = _dot(h, wg_ref[:, lo:hi])
        up = _dot(h, wu_ref[:, lo:hi])
        act = (gate * jax.nn.sigmoid(gate) * up).astype(BF16)
        part = _dot(act, wd_ref[lo:hi, :])
        acc = part if acc is None else acc + part
    return x + 0.5 * acc


def _const_spec(shape):
    nd = len(shape)
    return pl.BlockSpec(shape, lambda *_: (0,) * nd, pipeline_mode=pl.Buffered(1))


def _rows_in_kernel(x_ref, g_ffa_ref, wg_ref, wu_ref, wd_ref, g_mix_ref, w_in_ref,
                    x_out_ref, qkv_ref, hy_ref):
    x = _swiglu_residual(x_ref[...], g_ffa_ref[...], wg_ref, wu_ref, wd_ref)
    x_out_ref[...] = x
    h = (_rms(x) * g_mix_ref[...]).astype(BF16)
    nq = 3 * NA_WIDTH
    qkv_ref[...] = _dot(h, w_in_ref[:, :nq]).astype(BF16)
    hy_ref[...] = _dot(h, w_in_ref[:, nq:])


def _rows_in(x, g_ffa, wg, wu, wd, g_mix, w_in):
    m = x.shape[0]
    row = lambda w: pl.BlockSpec((ROW_TILE, w), lambda i: (i, 0))
    return pl.pallas_call(
        _rows_in_kernel,
        grid=(m // ROW_TILE,),
        in_specs=[row(D_MODEL), _const_spec(g_ffa.shape), _const_spec(wg.shape), _const_spec(wu.shape),
                  _const_spec(wd.shape), _const_spec(g_mix.shape), _const_spec(w_in.shape)],
        out_specs=[row(D_MODEL), row(3 * NA_WIDTH), row(3 * HY_WIDTH)],
        out_shape=[jax.ShapeDtypeStruct((m, D_MODEL), F32),
                   jax.ShapeDtypeStruct((m, 3 * NA_WIDTH), BF16),
                   jax.ShapeDtypeStruct((m, 3 * HY_WIDTH), F32)],
        compiler_params=pltpu.CompilerParams(dimension_semantics=("parallel",), vmem_limit_bytes=VMEM_LIMIT),
        name="rows_in",
    )(x, g_ffa, wg, wu, wd, g_mix, w_in)


def _rows_out_kernel(x_ref, yna_ref, yhy_ref, p_ref, g_out_ref, w_out_ref, g_ffb_ref, wg_ref, wu_ref, wd_ref,
                     g_ple_ref, w_pg_ref, w_pp_ref, g_fin_ref, o_ref, *, final):
    g_out = g_out_ref[...]
    y = jnp.concatenate([(yna_ref[...] * g_out[:, :NA_WIDTH]).astype(BF16),
                         (_rms(yhy_ref[...]) * g_out[:, NA_WIDTH:]).astype(BF16)], axis=-1)
    x = x_ref[...] + _dot(y, w_out_ref[...])
    x = _swiglu_residual(x, g_ffb_ref[...], wg_ref, wu_ref, wd_ref)
    h = (_rms(x) * g_ple_ref[...]).astype(BF16)
    gate = jax.nn.sigmoid(_dot(h, w_pg_ref[...]))
    x = x + gate * _dot(p_ref[...].astype(BF16), w_pp_ref[...])
    if final:
        x = _rms(x) * g_fin_ref[...]
    o_ref[...] = x


def _rows_out(x, yna, yhy, p, g_out, w_out, g_ffb, wg, wu, wd, g_ple, w_pg, w_pp, g_fin, final):
    m = x.shape[0]
    row = lambda w: pl.BlockSpec((ROW_TILE, w), lambda i: (i, 0))
    consts = (g_out, w_out, g_ffb, wg, wu, wd, g_ple, w_pg, w_pp, g_fin)
    return pl.pallas_call(
        functools.partial(_rows_out_kernel, final=final),
        grid=(m // ROW_TILE,),
        in_specs=[row(D_MODEL), row(NA_WIDTH), row(HY_WIDTH), row(PLE_DIM)] + [_const_spec(c.shape) for c in consts],
        out_specs=row(D_MODEL),
        out_shape=jax.ShapeDtypeStruct((m, D_MODEL), F32),
        compiler_params=pltpu.CompilerParams(dimension_semantics=("parallel",), vmem_limit_bytes=VMEM_LIMIT),
        name="rows_out",
    )(x, yna, yhy, p, *consts)


def _na_key_base(g, rows):
    return jnp.clip(NA_QROWS * g - NA_KH // 2, 0, rows - NA_KROWS)


def _na_bias_tables(rpb, rows):
    n_groups = rows // NA_QROWS
    tables = []
    qc = np.arange(GRID_W)
    cs = np.clip(qc - NA_KW // 2, 0, GRID_W - NA_KW)
    kc = np.arange(GRID_W)
    col_ok = (kc[None, :] >= cs[:, None]) & (kc[None, :] < cs[:, None] + NA_KW)
    dc_idx = np.clip(kc[None, :] - qc[:, None] + NA_KW - 1, 0, 2 * NA_KW - 2)
    dc_hot = jnp.asarray(np.eye(2 * NA_KW - 1, dtype=np.float32)[dc_idx])
    for g in (0, 1, n_groups - 1):
        r = NA_QROWS * g + np.arange(NA_QROWS)
        base = int(np.clip(NA_QROWS * g - NA_KH // 2, 0, rows - NA_KROWS))
        krow = base + np.arange(NA_KROWS)
        rs = np.clip(r - NA_KH // 2, 0, rows - NA_KH)
        row_ok = (krow[None, :] >= rs[:, None]) & (krow[None, :] < rs[:, None] + NA_KH)
        dr_idx = np.clip(krow[None, :] - r[:, None] + NA_KH - 1, 0, 2 * NA_KH - 2)
        dr_hot = jnp.asarray(np.eye(2 * NA_KH - 1, dtype=np.float32)[dr_idx])
        t = jnp.einsum("gka,hab->hgkb", dr_hot, rpb, precision=lax.Precision.HIGHEST)
        t = jnp.einsum("hgkb,qcb->hgqkc", t, dc_hot, precision=lax.Precision.HIGHEST)
        ok = jnp.asarray(row_ok[:, None, :, None] & col_ok[None, :, None, :])
        t = jnp.where(ok[None], t, -jnp.inf)
        tables.append(t.reshape(NA_HEADS, NA_QROWS * GRID_W, NA_KROWS * GRID_W))
    return jnp.stack(tables)


def _na_kernel(q_ref, k_ref, v_ref, bias_ref, o_ref, *, rows):
    g = pl.program_id(1)
    start = pl.multiple_of(_na_key_base(g, rows) * GRID_W, GRID_W)
    nk = NA_KROWS * GRID_W
    lane = lax.broadcasted_iota(jnp.int32, (1, LANES), 1)
    outs = []
    for pair in range(NA_HEADS // 2):
        cols = slice(pair * LANES, (pair + 1) * LANES)
        qp = q_ref[0, :, cols]
        kp = k_ref[0, pl.ds(start, nk), cols]
        vp = v_ref[0, pl.ds(start, nk), cols]
        acc = None
        for sub in range(2):
            mine = (lane < NA_HEAD_DIM) if sub == 0 else (lane >= NA_HEAD_DIM)
            qm = jnp.where(mine, qp, 0) * jnp.asarray(NA_HEAD_DIM ** -0.5, BF16)
            s = lax.dot_general(qm, kp, (((1,), (1,)), ((), ())), preferred_element_type=F32)
            s = s + bias_ref[0, 2 * pair + sub]
            e = jnp.exp(s - jnp.max(s, axis=-1, keepdims=True))
            denom = jnp.sum(e, axis=-1, keepdims=True)
            o = _dot(e.astype(BF16), jnp.where(mine, vp, 0)) * (1.0 / denom)
            acc = o if acc is None else acc + o
        outs.append(acc)
    o_ref[0] = _rms(jnp.concatenate(outs, axis=-1))


def _na(qkv, bias, batch, seq):
    rows = seq // GRID_W
    n_groups = rows // NA_QROWS
    tq = NA_QROWS * GRID_W
    tk = NA_KROWS * GRID_W

    def pattern(b, g):
        return (jnp.where(g == 0, 0, jnp.where(g == n_groups - 1, 2, 1)), 0, 0, 0)

    return pl.pallas_call(
        functools.partial(_na_kernel, rows=rows),
        grid=(batch, n_groups),
        in_specs=[pl.BlockSpec((1, tq, NA_WIDTH), lambda b, g: (b, g, 0)),
                  pl.BlockSpec((1, seq, NA_WIDTH), lambda b, g: (b, 0, 1)),
                  pl.BlockSpec((1, seq, NA_WIDTH), lambda b, g: (b, 0, 2)),
                  pl.BlockSpec((1, NA_HEADS, tq, tk), pattern)],
        out_specs=pl.BlockSpec((1, tq, NA_WIDTH), lambda b, g: (b, g, 0)),
        out_shape=jax.ShapeDtypeStruct((batch, seq, NA_WIDTH), F32),
        compiler_params=pltpu.CompilerParams(dimension_semantics=("parallel", "arbitrary"),
                                             vmem_limit_bytes=VMEM_LIMIT),
        name="na",
    )(qkv, qkv, qkv, bias)


def _dft_mats(seq):
    n = jnp.arange(seq, dtype=jnp.int32)
    idx = (n[:, None] * n[None, :]) % (2 * seq)
    ang = idx.astype(F32) * (math.pi / seq)
    return jnp.cos(ang).astype(BF16), jnp.sin(ang).astype(BF16)


def _filter_features(seq):
    bands = (FILTER_EMB - 1) // 2
    t = jnp.linspace(0.0, 1.0, seq, dtype=F32)[:, None]
    w = 2.0 * math.pi * jnp.arange(seq, dtype=F32)[:, None] / seq
    f = jnp.linspace(1e-4, bands - 1, bands, dtype=F32)[None, :]
    z = jnp.concatenate([t, jnp.cos(f * w), -jnp.sin(f * w)], axis=-1)
    return t, jnp.pad(z, ((0, 0), (0, LANES - FILTER_EMB)))


def _filt_kernel(z_ref, w1_ref, b1_ref, w2_ref, b2_ref, w3_ref, b3_ref, w4_ref, fr_ref, t_ref, dl_ref,
                 c_ref, s_ref, kr_ref, ki_ref, kn_ref):
    seq = z_ref.shape[0]
    hp = lax.Precision.HIGHEST
    fr = fr_ref[...]
    h = jnp.sin(fr * (jnp.dot(z_ref[...], w1_ref[...], precision=hp, preferred_element_type=F32) + b1_ref[...]))
    h = jnp.sin(fr * (jnp.dot(h, w2_ref[...], precision=hp, preferred_element_type=F32) + b2_ref[...]))
    h = jnp.sin(fr * (jnp.dot(h, w3_ref[...], precision=hp, preferred_element_type=F32) + b3_ref[...]))
    h = jnp.dot(h, w4_ref[...], precision=hp, preferred_element_type=F32)
    decay = jnp.exp(-t_ref[...] * dl_ref[...])
    hf = h[:, :HY_WIDTH] * decay
    hb = h[:, HY_WIDTH:] * decay
    row = lax.broadcasted_iota(jnp.int32, (seq, 1), 0)
    first = row == 0
    se = hf + hb
    norm = jnp.sum(jnp.abs(hf) + jnp.abs(hb) + jnp.where(first, jnp.abs(se) - jnp.abs(hf) - jnp.abs(hb), 0.0),
                   axis=0, keepdims=True)
    inv = 1.0 / norm
    sign = jnp.where((row & 1) == 0, 1.0, -1.0)
    scale = inv * jnp.where(first, 1.0 / (2 * seq), 2.0 / (2 * seq))
    kr_ref[...] = _dot(c_ref[...], se.astype(BF16)) * scale
    ki_ref[...] = _dot(s_ref[...], (hb - hf).astype(BF16)) * scale
    kn_ref[...] = jnp.sum(se * sign, axis=0, keepdims=True) * (inv * (1.0 / (2 * seq)))


def _filter_spectrum(z, t, deltas, cmat, smat, w1, b1, w2, b2, w3, b3, w4, freq):
    seq = z.shape[0]
    pad2 = lambda w, r, c: jnp.pad(w, ((0, r - w.shape[0]), (0, c - w.shape[1])))
    args = (z, pad2(w1, LANES, LANES), pad2(b1[None], 1, LANES), pad2(w2, LANES, LANES), pad2(b2[None], 1, LANES),
            pad2(w3, LANES, LANES), pad2(b3[None], 1, LANES), pad2(w4, LANES, 2 * HY_WIDTH),
            pad2(freq[None], 1, LANES), t, deltas, cmat, smat)
    return pl.pallas_call(
        _filt_kernel,
        out_shape=[jax.ShapeDtypeStruct((seq, HY_WIDTH), F32), jax.ShapeDtypeStruct((seq, HY_WIDTH), F32),
                   jax.ShapeDtypeStruct((1, HY_WIDTH), F32)],
        compiler_params=pltpu.CompilerParams(vmem_limit_bytes=VMEM_LIMIT),
        name="filt",
    )(*args)


def _short_conv(u, w_ref, b_ref, first, last):
    seq = u.shape[0]
    prev = jnp.where(first, 0.0, pltpu.roll(u, 1, 0))
    nxt = jnp.where(last, 0.0, pltpu.roll(u, seq - 1, 0))
    return prev * w_ref[0:1, :] + u * w_ref[1:2, :] + nxt * w_ref[2:3, :] + b_ref[...]


def _hyena_kernel(x0_ref, x1_ref, hv_ref, w0_ref, w1_ref, wv_ref, b0_ref, b1_ref, bv_ref,
                  c_ref, s_ref, kr_ref, ki_ref, kn_ref, hb_ref, o_ref, v_ref, vb_ref, yr_ref, yi_ref):
    seq = x0_ref.shape[1]
    row = lax.broadcasted_iota(jnp.int32, (seq, 1), 0)
    first = row == 0
    last = row == seq - 1
    sign = jnp.where((row & 1) == 0, 1.0, -1.0)
    v = _short_conv(hv_ref[0], wv_ref, bv_ref, first, last) * _short_conv(x1_ref[0], w1_ref, b1_ref, first, last)
    v_ref[...] = v
    vb_ref[...] = v.astype(BF16)
    vn = jnp.sum(v * sign, axis=0, keepdims=True) * kn_ref[...]
    o_ref[0] = _short_conv(x0_ref[0], w0_ref, b0_ref, first, last)

    def forward(i, carry):
        rs = pl.ds(pl.multiple_of(i * HY_RCHUNK, HY_RCHUNK), HY_RCHUNK)
        a = _dot(c_ref[rs, :], vb_ref[...])
        b = _dot(s_ref[rs, :], vb_ref[...])
        kr = kr_ref[rs, :]
        ki = ki_ref[rs, :]
        yr_ref[rs, :] = (a * kr + b * ki).astype(BF16)
        yi_ref[rs, :] = (a * ki - b * kr).astype(BF16)
        return carry

    def inverse(i, carry):
        rs = pl.ds(pl.multiple_of(i * HY_RCHUNK, HY_RCHUNK), HY_RCHUNK)
        r = i * HY_RCHUNK + lax.broadcasted_iota(jnp.int32, (HY_RCHUNK, 1), 0)
        y = _dot(c_ref[rs, :], yr_ref[...]) - _dot(s_ref[rs, :], yi_ref[...])
        y = y + jnp.where((r & 1) == 0, vn, -vn)
        o_ref[0, rs, :] = o_ref[0, rs, :] * (y + v_ref[rs, :] * hb_ref[...])
        return carry

    lax.fori_loop(0, seq // HY_RCHUNK, forward, 0)
    lax.fori_loop(0, seq // HY_RCHUNK, inverse, 0)


def _hyena(hy, w_sc, b_sc, cmat, smat, kr, ki, kn, hy_bias, batch, seq):
    nt = HY_WIDTH // HY_CTILE
    blk = lambda off: pl.BlockSpec((1, seq, HY_CTILE), lambda j, b: (b, 0, off * nt + j))
    wblk = lambda off: pl.BlockSpec((3, HY_CTILE), lambda j, b: (0, off * nt + j))
    bblk = lambda off: pl.BlockSpec((1, HY_CTILE), lambda j, b: (0, off * nt + j))
    chan = lambda r: pl.BlockSpec((r, HY_CTILE), lambda j, b: (0, j))
    spec = pl.BlockSpec((seq, HY_CTILE), lambda j, b: (0, j), pipeline_mode=pl.Buffered(1))
    return pl.pallas_call(
        _hyena_kernel,
        grid=(nt, batch),
        in_specs=[blk(0), blk(1), blk(2), wblk(0), wblk(1), wblk(2), bblk(0), bblk(1), bblk(2),
                  _const_spec(cmat.shape), _const_spec(smat.shape), spec, spec, chan(1), chan(1)],
        out_specs=pl.BlockSpec((1, seq, HY_CTILE), lambda j, b: (b, 0, j)),
        out_shape=jax.ShapeDtypeStruct((batch, seq, HY_WIDTH), F32),
        scratch_shapes=[pltpu.VMEM((seq, HY_CTILE), F32), pltpu.VMEM((seq, HY_CTILE), BF16),
                        pltpu.VMEM((seq, HY_CTILE), BF16), pltpu.VMEM((seq, HY_CTILE), BF16)],
        compiler_params=pltpu.CompilerParams(dimension_semantics=("arbitrary", "arbitrary"),
                                             vmem_limit_bytes=VMEM_LIMIT),
        name="hyena",
    )(hy, hy, hy, w_sc, w_sc, w_sc, b_sc, b_sc, b_sc, cmat, smat, kr, ki, kn, hy_bias)


def kernel(x, p, g_ffa, w_ffa_gate, w_ffa_up, w_ffa_down, g_mix, w_in, na_rpb, w_sc, b_sc, w_f1, b_f1, w_f2, b_f2, w_f3, b_f3, w_f4, filt_freq, hy_bias, g_out, w_out, g_ffb, w_ffb_gate, w_ffb_up, w_ffb_down, g_ple, w_ple_gate, w_ple_proj, g_final):
    batch, seq, _ = x.shape
    depth = p.shape[0]
    m = batch * seq
    rows = seq // GRID_W
    bf = lambda w: w.astype(BF16)

    cmat, smat = _dft_mats(seq)
    t, z = _filter_features(seq)
    deltas = jnp.abs(jnp.linspace(MIN_DECAY, MAX_DECAY, HY_WIDTH, dtype=F32))[None, :]

    xr = x.reshape(m, D_MODEL)
    for i in range(depth):
        xr, qkv, hy = _rows_in(xr, g_ffa[i][None], bf(w_ffa_gate[i]), bf(w_ffa_up[i]), bf(w_ffa_down[i]),
                               g_mix[i][None], bf(w_in[i]))
        bias = _na_bias_tables(na_rpb[i], rows)
        yna = _na(qkv.reshape(batch, seq, 3 * NA_WIDTH), bias, batch, seq)
        kr, ki, kn = _filter_spectrum(z, t, deltas, cmat, smat, w_f1[i], b_f1[i], w_f2[i], b_f2[i],
                                      w_f3[i], b_f3[i], w_f4[i], filt_freq[i])
        yhy = _hyena(hy.reshape(batch, seq, 3 * HY_WIDTH), w_sc[i], b_sc[i][None], cmat, smat, kr, ki, kn,
                     hy_bias[i][None], batch, seq)
        xr = _rows_out(xr, yna.reshape(m, NA_WIDTH), yhy.reshape(m, HY_WIDTH), p[i].reshape(m, PLE_DIM),
                       g_out[i][None], bf(w_out[i]), g_ffb[i][None], bf(w_ffb_gate[i]), bf(w_ffb_up[i]),
                       bf(w_ffb_down[i]), g_ple[i][None], bf(w_ple_gate[i]), bf(w_ple_proj[i]),
                       g_final[None], final=(i == depth - 1))
    return xr.reshape(batch, seq, D_MODEL)
```

```python
import functools
import math

import numpy as np
import jax
import jax.numpy as jnp
from jax import lax
from jax.experimental import pallas as pl
from jax.experimental.pallas import tpu as pltpu

F32 = jnp.float32
BF16 = jnp.bfloat16

D_MODEL = 1024
GRID_W = 64
NA_HEADS = 8
NA_HEAD_DIM = 64
NA_WIDTH = NA_HEADS * NA_HEAD_DIM
NA_KH = 8
NA_KW = 16
HY_WIDTH = D_MODEL - NA_WIDTH
FILTER_EMB = 33
DECAY_TARGET = 1e-2
FAST_DECAY_PCT = 0.3
SLOW_DECAY_PCT = 1.5
MAX_DECAY = math.log(DECAY_TARGET) / FAST_DECAY_PCT
MIN_DECAY = math.log(DECAY_TARGET) / SLOW_DECAY_PCT
D_FF = 2816
PLE_DIM = 256
EPS = 1e-6

LANES = 128
ROW_TILE = 256
FF_CHUNKS = ((0, 1024), (1024, 2048), (2048, D_FF))
NA_QROWS = 4
NA_KROWS = 12
HY_CTILE = 256
HY_RCHUNK = 512
DFT_SPLIT = 64
VMEM_LIMIT = 56 * 1024 * 1024


def _rms(x):
    return x * lax.rsqrt(jnp.mean(x * x, axis=-1, keepdims=True) + EPS)


def _dot(a, b):
    return jnp.dot(a, b, preferred_element_type=F32)


def _swiglu_residual(x, g, wg_ref, wu_ref, wd_ref):
    h = (_rms(x) * g).astype(BF16)
    acc = None
    for lo, hi in FF_CHUNKS:
        gate = _dot(h, wg_ref[:, lo:hi])
        up = _dot(h, wu_ref[:, lo:hi])
        act = (gate * jax.nn.sigmoid(gate) * up).astype(BF16)
        part = _dot(act, wd_ref[lo:hi, :])
        acc = part if acc is None else acc + part
    return x + 0.5 * acc


def _const_spec(shape):
    nd = len(shape)
    return pl.BlockSpec(shape, lambda *_: (0,) * nd, pipeline_mode=pl.Buffered(1))


def _layer_spec(stacked, layer):
    return pl.BlockSpec((None,) + stacked.shape[1:], lambda *_: (layer, 0, 0), pipeline_mode=pl.Buffered(1))


def _rows_in_kernel(x_ref, g_ffa_ref, wg_ref, wu_ref, wd_ref, g_mix_ref, w_in_ref,
                    x_out_ref, qkv_ref, hy_ref):
    x = _swiglu_residual(x_ref[...], g_ffa_ref[...], wg_ref, wu_ref, wd_ref)
    x_out_ref[...] = x
    h = (_rms(x) * g_mix_ref[...]).astype(BF16)
    nq = 3 * NA_WIDTH
    qkv_ref[...] = _dot(h, w_in_ref[:, :nq]).astype(BF16)
    hy_ref[...] = _dot(h, w_in_ref[:, nq:])


def _rows_in(x, layer, g_ffa, wg, wu, wd, g_mix, w_in):
    m = x.shape[0]
    row = lambda w: pl.BlockSpec((ROW_TILE, w), lambda i: (i, 0))
    return pl.pallas_call(
        _rows_in_kernel,
        grid=(m // ROW_TILE,),
        in_specs=[row(D_MODEL)] + [_layer_spec(c, layer) for c in (g_ffa, wg, wu, wd, g_mix, w_in)],
        out_specs=[row(D_MODEL), row(3 * NA_WIDTH), row(3 * HY_WIDTH)],
        out_shape=[jax.ShapeDtypeStruct((m, D_MODEL), F32),
                   jax.ShapeDtypeStruct((m, 3 * NA_WIDTH), BF16),
                   jax.ShapeDtypeStruct((m, 3 * HY_WIDTH), F32)],
        compiler_params=pltpu.CompilerParams(dimension_semantics=("parallel",), vmem_limit_bytes=VMEM_LIMIT),
        name="rows_in",
    )(x, g_ffa, wg, wu, wd, g_mix, w_in)


def _rows_out_kernel(x_ref, yna_ref, yhy_ref, p_ref, g_out_ref, w_out_ref, g_ffb_ref, wg_ref, wu_ref, wd_ref,
                     g_ple_ref, w_pg_ref, w_pp_ref, g_fin_ref, o_ref, *, final):
    g_out = g_out_ref[...]
    y = jnp.concatenate([(yna_ref[...] * g_out[:, :NA_WIDTH]).astype(BF16),
                         (_rms(yhy_ref[...]) * g_out[:, NA_WIDTH:]).astype(BF16)], axis=-1)
    x = x_ref[...] + _dot(y, w_out_ref[...])
    x = _swiglu_residual(x, g_ffb_ref[...], wg_ref, wu_ref, wd_ref)
    h = (_rms(x) * g_ple_ref[...]).astype(BF16)
    gate = jax.nn.sigmoid(_dot(h, w_pg_ref[...]))
    x = x + gate * _dot(p_ref[...].astype(BF16), w_pp_ref[...])
    if final:
        x = _rms(x) * g_fin_ref[...]
    o_ref[...] = x


def _rows_out(x, yna, yhy, p, layer, g_out, w_out, g_ffb, wg, wu, wd, g_ple, w_pg, w_pp, g_fin, final):
    m = x.shape[0]
    row = lambda w: pl.BlockSpec((ROW_TILE, w), lambda i: (i, 0))
    stacks = (g_out, w_out, g_ffb, wg, wu, wd, g_ple, w_pg, w_pp)
    return pl.pallas_call(
        functools.partial(_rows_out_kernel, final=final),
        grid=(m // ROW_TILE,),
        in_specs=[row(D_MODEL), row(NA_WIDTH), row(HY_WIDTH),
                  pl.BlockSpec((None, ROW_TILE, PLE_DIM), lambda i: (layer, i, 0))]
                 + [_layer_spec(c, layer) for c in stacks] + [_const_spec(g_fin.shape)],
        out_specs=row(D_MODEL),
        out_shape=jax.ShapeDtypeStruct((m, D_MODEL), F32),
        compiler_params=pltpu.CompilerParams(dimension_semantics=("parallel",), vmem_limit_bytes=VMEM_LIMIT),
        name="rows_out",
    )(x, yna, yhy, p, *stacks, g_fin)


def _na_key_base(g, rows):
    return jnp.clip(NA_QROWS * g - NA_KH // 2, 0, rows - NA_KROWS)


def _na_bias_tables(rpb, rows):
    n_groups = rows // NA_QROWS
    tables = []
    qc = np.arange(GRID_W)
    cs = np.clip(qc - NA_KW // 2, 0, GRID_W - NA_KW)
    kc = np.arange(GRID_W)
    col_ok = (kc[None, :] >= cs[:, None]) & (kc[None, :] < cs[:, None] + NA_KW)
    dc_idx = np.clip(kc[None, :] - qc[:, None] + NA_KW - 1, 0, 2 * NA_KW - 2)
    dc_hot = jnp.asarray(np.eye(2 * NA_KW - 1, dtype=np.float32)[dc_idx])
    blocks = jnp.einsum("hab,qcb->haqc", rpb, dc_hot, precision=lax.Precision.HIGHEST)
    blocks = jnp.where(jnp.asarray(col_ok)[None, None], blocks, -jnp.inf)
    hidden = jnp.full((NA_HEADS, GRID_W, GRID_W), -jnp.inf, F32)
    for g in (0, 1, n_groups - 1):
        r = NA_QROWS * g + np.arange(NA_QROWS)
        base = int(np.clip(NA_QROWS * g - NA_KH // 2, 0, rows - NA_KROWS))
        krow = base + np.arange(NA_KROWS)
        rs = np.clip(r - NA_KH // 2, 0, rows - NA_KH)
        row_ok = (krow[None, :] >= rs[:, None]) & (krow[None, :] < rs[:, None] + NA_KH)
        dr_idx = krow[None, :] - r[:, None] + NA_KH - 1
        strips = [jnp.concatenate([blocks[:, dr_idx[q, k]] if row_ok[q, k] else hidden for k in range(NA_KROWS)],
                                  axis=-1) for q in range(NA_QROWS)]
        tables.append(jnp.concatenate(strips, axis=1))
    return jnp.stack(tables)


def _na_kernel(q_ref, k_ref, v_ref, bias_ref, o_ref, *, rows):
    g = pl.program_id(1)
    start = pl.multiple_of(_na_key_base(g, rows) * GRID_W, GRID_W)
    nk = NA_KROWS * GRID_W
    lane = lax.broadcasted_iota(jnp.int32, (1, LANES), 1)
    outs = []
    for pair in range(NA_HEADS // 2):
        cols = slice(pair * LANES, (pair + 1) * LANES)
        qp = q_ref[0, :, cols]
        kp = k_ref[0, pl.ds(start, nk), cols]
        vp = v_ref[0, pl.ds(start, nk), cols]
        acc = None
        for sub in range(2):
            mine = (lane < NA_HEAD_DIM) if sub == 0 else (lane >= NA_HEAD_DIM)
            qm = jnp.where(mine, qp, 0) * jnp.asarray(NA_HEAD_DIM ** -0.5, BF16)
            s = lax.dot_general(qm, kp, (((1,), (1,)), ((), ())), preferred_element_type=F32)
            s = s + bias_ref[0, 2 * pair + sub]
            e = jnp.exp(s - jnp.max(s, axis=-1, keepdims=True))
            denom = jnp.sum(e, axis=-1, keepdims=True)
            o = _dot(e.astype(BF16), jnp.where(mine, vp, 0)) * (1.0 / denom)
            acc = o if acc is None else acc + o
        outs.append(acc)
    o_ref[0] = _rms(jnp.concatenate(outs, axis=-1))


def _na(qkv, bias, batch, seq):
    rows = seq // GRID_W
    n_groups = rows // NA_QROWS
    tq = NA_QROWS * GRID_W
    tk = NA_KROWS * GRID_W

    def pattern(b, g):
        return (jnp.where(g == 0, 0, jnp.where(g == n_groups - 1, 2, 1)), 0, 0, 0)

    return pl.pallas_call(
        functools.partial(_na_kernel, rows=rows),
        grid=(batch, n_groups),
        in_specs=[pl.BlockSpec((1, tq, NA_WIDTH), lambda b, g: (b, g, 0)),
                  pl.BlockSpec((1, seq, NA_WIDTH), lambda b, g: (b, 0, 1)),
                  pl.BlockSpec((1, seq, NA_WIDTH), lambda b, g: (b, 0, 2)),
                  pl.BlockSpec((1, NA_HEADS, tq, tk), pattern)],
        out_specs=pl.BlockSpec((1, tq, NA_WIDTH), lambda b, g: (b, g, 0)),
        out_shape=jax.ShapeDtypeStruct((batch, seq, NA_WIDTH), F32),
        compiler_params=pltpu.CompilerParams(dimension_semantics=("parallel", "arbitrary"),
                                             vmem_limit_bytes=VMEM_LIMIT),
        name="na",
    )(qkv, qkv, qkv, bias)


def _dft_mats(seq):
    n = jnp.arange(seq, dtype=jnp.int32)[None, :]
    a = jnp.arange(seq // DFT_SPLIT, dtype=jnp.int32)[:, None]
    b = jnp.arange(DFT_SPLIT, dtype=jnp.int32)[:, None]
    ang_a = ((DFT_SPLIT * a * n) % (2 * seq)).astype(F32) * (math.pi / seq)
    ang_b = ((b * n) % (2 * seq)).astype(F32) * (math.pi / seq)
    ca, sa = jnp.cos(ang_a)[:, None, :], jnp.sin(ang_a)[:, None, :]
    cb, sb = jnp.cos(ang_b)[None, :, :], jnp.sin(ang_b)[None, :, :]
    cmat = (ca * cb - sa * sb).reshape(seq, seq)
    smat = (sa * cb + ca * sb).reshape(seq, seq)
    return cmat.astype(BF16), smat.astype(BF16)


def _filter_features(seq):
    bands = (FILTER_EMB - 1) // 2
    t = jnp.linspace(0.0, 1.0, seq, dtype=F32)[:, None]
    w = 2.0 * math.pi * jnp.arange(seq, dtype=F32)[:, None] / seq
    f = jnp.linspace(1e-4, bands - 1, bands, dtype=F32)[None, :]
    z = jnp.concatenate([t, jnp.cos(f * w), -jnp.sin(f * w)], axis=-1)
    return t, jnp.pad(z, ((0, 0), (0, LANES - FILTER_EMB)))


def _filt_kernel(z_ref, w1_ref, b1_ref, w2_ref, b2_ref, w3_ref, b3_ref, w4_ref, fr_ref, t_ref, dl_ref,
                 c_ref, s_ref, kr_ref, ki_ref, kn_ref):
    seq = z_ref.shape[0]
    hp = lax.Precision.HIGHEST
    fr = fr_ref[...]
    h = jnp.sin(fr * (jnp.dot(z_ref[...], w1_ref[...], precision=hp, preferred_element_type=F32) + b1_ref[...]))
    h = jnp.sin(fr * (jnp.dot(h, w2_ref[...], precision=hp, preferred_element_type=F32) + b2_ref[...]))
    h = jnp.sin(fr * (jnp.dot(h, w3_ref[...], precision=hp, preferred_element_type=F32) + b3_ref[...]))
    h = jnp.dot(h, w4_ref[...], precision=hp, preferred_element_type=F32)
    decay = jnp.exp(-t_ref[...] * dl_ref[...])
    hf = h[:, :HY_WIDTH] * decay
    hb = h[:, HY_WIDTH:] * decay
    row = lax.broadcasted_iota(jnp.int32, (seq, 1), 0)
    first = row == 0
    se = hf + hb
    norm = jnp.sum(jnp.abs(hf) + jnp.abs(hb) + jnp.where(first, jnp.abs(se) - jnp.abs(hf) - jnp.abs(hb), 0.0),
                   axis=0, keepdims=True)
    inv = 1.0 / norm
    sign = jnp.where((row & 1) == 0, 1.0, -1.0)
    scale = inv * jnp.where(first, 1.0 / (2 * seq), 2.0 / (2 * seq))
    kr_ref[...] = _dot(c_ref[...], se.astype(BF16)) * scale
    ki_ref[...] = _dot(s_ref[...], (hb - hf).astype(BF16)) * scale
    kn_ref[...] = jnp.sum(se * sign, axis=0, keepdims=True) * (inv * (1.0 / (2 * seq)))


def _filter_spectrum(z, t, deltas, cmat, smat, w1, b1, w2, b2, w3, b3, w4, freq):
    seq = z.shape[0]
    pad2 = lambda w, r, c: jnp.pad(w, ((0, r - w.shape[0]), (0, c - w.shape[1])))
    args = (z, pad2(w1, LANES, LANES), pad2(b1[None], 1, LANES), pad2(w2, LANES, LANES), pad2(b2[None], 1, LANES),
            pad2(w3, LANES, LANES), pad2(b3[None], 1, LANES), pad2(w4, LANES, 2 * HY_WIDTH),
            pad2(freq[None], 1, LANES), t, deltas, cmat, smat)
    return pl.pallas_call(
        _filt_kernel,
        out_shape=[jax.ShapeDtypeStruct((seq, HY_WIDTH), F32), jax.ShapeDtypeStruct((seq, HY_WIDTH), F32),
                   jax.ShapeDtypeStruct((1, HY_WIDTH), F32)],
        compiler_params=pltpu.CompilerParams(vmem_limit_bytes=VMEM_LIMIT),
        name="filt",
    )(*args)


def _short_conv(u, w_ref, b_ref, first, last):
    seq = u.shape[0]
    prev = jnp.where(first, 0.0, pltpu.roll(u, 1, 0))
    nxt = jnp.where(last, 0.0, pltpu.roll(u, seq - 1, 0))
    return prev * w_ref[0:1, :] + u * w_ref[1:2, :] + nxt * w_ref[2:3, :] + b_ref[...]


def _hyena_kernel(x0_ref, x1_ref, hv_ref, w0_ref, w1_ref, wv_ref, b0_ref, b1_ref, bv_ref,
                  c_ref, s_ref, kr_ref, ki_ref, kn_ref, hb_ref, o_ref, v_ref, vb_ref, yr_ref, yi_ref):
    seq = x0_ref.shape[1]
    row = lax.broadcasted_iota(jnp.int32, (seq, 1), 0)
    first = row == 0
    last = row == seq - 1
    sign = jnp.where((row & 1) == 0, 1.0, -1.0)
    v = _short_conv(hv_ref[0], wv_ref, bv_ref, first, last) * _short_conv(x1_ref[0], w1_ref, b1_ref, first, last)
    v_ref[...] = v
    vb_ref[...] = v.astype(BF16)
    vn = jnp.sum(v * sign, axis=0, keepdims=True) * kn_ref[...]
    o_ref[0] = _short_conv(x0_ref[0], w0_ref, b0_ref, first, last)

    def forward(i, carry):
        rs = pl.ds(pl.multiple_of(i * HY_RCHUNK, HY_RCHUNK), HY_RCHUNK)
        a = _dot(c_ref[rs, :], vb_ref[...])
        b = _dot(s_ref[rs, :], vb_ref[...])
        kr = kr_ref[rs, :]
        ki = ki_ref[rs, :]
        yr_ref[rs, :] = (a * kr + b * ki).astype(BF16)
        yi_ref[rs, :] = (a * ki - b * kr).astype(BF16)
        return carry

    def inverse(i, carry):
        rs = pl.ds(pl.multiple_of(i * HY_RCHUNK, HY_RCHUNK), HY_RCHUNK)
        r = i * HY_RCHUNK + lax.broadcasted_iota(jnp.int32, (HY_RCHUNK, 1), 0)
        y = _dot(c_ref[rs, :], yr_ref[...]) - _dot(s_ref[rs, :], yi_ref[...])
        y = y + jnp.where((r & 1) == 0, vn, -vn)
        o_ref[0, rs, :] = o_ref[0, rs, :] * (y + v_ref[rs, :] * hb_ref[...])
        return carry

    lax.fori_loop(0, seq // HY_RCHUNK, forward, 0)
    lax.fori_loop(0, seq // HY_RCHUNK, inverse, 0)


def _hyena(hy, w_sc, b_sc, cmat, smat, kr, ki, kn, hy_bias, batch, seq):
    nt = HY_WIDTH // HY_CTILE
    blk = lambda off: pl.BlockSpec((1, seq, HY_CTILE), lambda j, b: (b, 0, off * nt + j))
    wblk = lambda off: pl.BlockSpec((3, HY_CTILE), lambda j, b: (0, off * nt + j))
    bblk = lambda off: pl.BlockSpec((1, HY_CTILE), lambda j, b: (0, off * nt + j))
    chan = lambda r: pl.BlockSpec((r, HY_CTILE), lambda j, b: (0, j))
    spec = pl.BlockSpec((seq, HY_CTILE), lambda j, b: (0, j), pipeline_mode=pl.Buffered(1))
    return pl.pallas_call(
        _hyena_kernel,
        grid=(nt, batch),
        in_specs=[blk(0), blk(1), blk(2), wblk(0), wblk(1), wblk(2), bblk(0), bblk(1), bblk(2),
                  _const_spec(cmat.shape), _const_spec(smat.shape), spec, spec, chan(1), chan(1)],
        out_specs=pl.BlockSpec((1, seq, HY_CTILE), lambda j, b: (b, 0, j)),
        out_shape=jax.ShapeDtypeStruct((batch, seq, HY_WIDTH), F32),
        scratch_shapes=[pltpu.VMEM((seq, HY_CTILE), F32), pltpu.VMEM((seq, HY_CTILE), BF16),
                        pltpu.VMEM((seq, HY_CTILE), BF16), pltpu.VMEM((seq, HY_CTILE), BF16)],
        compiler_params=pltpu.CompilerParams(dimension_semantics=("arbitrary", "arbitrary"),
                                             vmem_limit_bytes=VMEM_LIMIT),
        name="hyena",
    )(hy, hy, hy, w_sc, w_sc, w_sc, b_sc, b_sc, b_sc, cmat, smat, kr, ki, kn, hy_bias)


def kernel(x, p, g_ffa, w_ffa_gate, w_ffa_up, w_ffa_down, g_mix, w_in, na_rpb, w_sc, b_sc, w_f1, b_f1, w_f2, b_f2, w_f3, b_f3, w_f4, filt_freq, hy_bias, g_out, w_out, g_ffb, w_ffb_gate, w_ffb_up, w_ffb_down, g_ple, w_ple_gate, w_ple_proj, g_final):
    batch, seq, _ = x.shape
    depth = p.shape[0]
    m = batch * seq
    rows = seq // GRID_W
    bf = lambda w: w.astype(BF16)
    vec = lambda g: g[:, None, :]

    cmat, smat = _dft_mats(seq)
    t, z = _filter_features(seq)
    deltas = jnp.abs(jnp.linspace(MIN_DECAY, MAX_DECAY, HY_WIDTH, dtype=F32))[None, :]

    in_params = (vec(g_ffa), bf(w_ffa_gate), bf(w_ffa_up), bf(w_ffa_down), vec(g_mix), bf(w_in))
    out_params = (vec(g_out), bf(w_out), vec(g_ffb), bf(w_ffb_gate), bf(w_ffb_up), bf(w_ffb_down),
                  vec(g_ple), bf(w_ple_gate), bf(w_ple_proj))
    pr = p.reshape(depth, m, PLE_DIM)

    xr = x.reshape(m, D_MODEL)
    for i in range(depth):
        xr, qkv, hy = _rows_in(xr, i, *in_params)
        bias = _na_bias_tables(na_rpb[i], rows)
        yna = _na(qkv.reshape(batch, seq, 3 * NA_WIDTH), bias, batch, seq)
        kr, ki, kn = _filter_spectrum(z, t, deltas, cmat, smat, w_f1[i], b_f1[i], w_f2[i], b_f2[i],
                                      w_f3[i], b_f3[i], w_f4[i], filt_freq[i])
        yhy = _hyena(hy.reshape(batch, seq, 3 * HY_WIDTH), w_sc[i], b_sc[i][None], cmat, smat, kr, ki, kn,
                     hy_bias[i][None], batch, seq)
        xr = _rows_out(xr, yna.reshape(m, NA_WIDTH), yhy.reshape(m, HY_WIDTH), pr, i, *out_params,
                       g_final[None], final=(i == depth - 1))
    return xr.reshape(batch, seq, D_MODEL)
```

```python
import functools
import math

import numpy as np
import jax
import jax.numpy as jnp
from jax import lax
from jax.experimental import pallas as pl
from jax.experimental.pallas import tpu as pltpu

F32 = jnp.float32
BF16 = jnp.bfloat16

D_MODEL = 1024
GRID_W = 64
NA_HEADS = 8
NA_HEAD_DIM = 64
NA_WIDTH = NA_HEADS * NA_HEAD_DIM
NA_KH = 8
NA_KW = 16
HY_WIDTH = D_MODEL - NA_WIDTH
FILTER_EMB = 33
DECAY_TARGET = 1e-2
FAST_DECAY_PCT = 0.3
SLOW_DECAY_PCT = 1.5
MAX_DECAY = math.log(DECAY_TARGET) / FAST_DECAY_PCT
MIN_DECAY = math.log(DECAY_TARGET) / SLOW_DECAY_PCT
D_FF = 2816
PLE_DIM = 256
EPS = 1e-6

LANES = 128
ROW_TILE = 256
FF_CHUNKS = ((0, 1024), (1024, 2048), (2048, D_FF))
NA_QROWS = 4
NA_KROWS = 12
HY_CTILE = 256
DFT_SPLIT = 64
VMEM_LIMIT = 56 * 1024 * 1024


def _rms(x):
    return x * lax.rsqrt(jnp.mean(x * x, axis=-1, keepdims=True) + EPS)


def _dot(a, b):
    return jnp.dot(a, b, preferred_element_type=F32)


def _swiglu_residual(x, g, wg_ref, wu_ref, wd_ref):
    h = (_rms(x) * g).astype(BF16)
    acc = None
    for lo, hi in FF_CHUNKS:
        gate = _dot(h, wg_ref[:, lo:hi])
        up = _dot(h, wu_ref[:, lo:hi])
        act = (gate * jax.nn.sigmoid(gate) * up).astype(BF16)
        part = _dot(act, wd_ref[lo:hi, :])
        acc = part if acc is None else acc + part
    return x + 0.5 * acc


def _const_spec(shape):
    nd = len(shape)
    return pl.BlockSpec(shape, lambda *_: (0,) * nd, pipeline_mode=pl.Buffered(1))


def _layer_spec(stacked, layer):
    return pl.BlockSpec((None,) + stacked.shape[1:], lambda *_: (layer, 0, 0), pipeline_mode=pl.Buffered(1))


def _pair_spec(width):
    return pl.BlockSpec((ROW_TILE, width), lambda i, par: (i, par))


def _split_spec(width):
    return pl.BlockSpec((None, ROW_TILE, width), lambda i, par: (par, i, 0))


def _rows_in_kernel(x_ref, g_ffa_ref, wg_ref, wu_ref, wd_ref, g_mix_ref, w_in_ref,
                    x_out_ref, qkv_ref, hy_ref):
    x = _swiglu_residual(x_ref[...], g_ffa_ref[...], wg_ref, wu_ref, wd_ref)
    x_out_ref[...] = x
    h = (_rms(x) * g_mix_ref[...]).astype(BF16)
    nq = 3 * NA_WIDTH
    qkv_ref[...] = _dot(h, w_in_ref[:, :nq]).astype(BF16)
    hy_ref[...] = _dot(h, w_in_ref[:, nq:])


def _rows_in(x, layer, g_ffa, wg, wu, wd, g_mix, w_in):
    m2 = x.shape[0]
    return pl.pallas_call(
        _rows_in_kernel,
        grid=(m2 // ROW_TILE, 2),
        in_specs=[_pair_spec(D_MODEL)] + [_layer_spec(c, layer) for c in (g_ffa, wg, wu, wd, g_mix, w_in)],
        out_specs=[_pair_spec(D_MODEL), _pair_spec(3 * NA_WIDTH), _split_spec(3 * HY_WIDTH)],
        out_shape=[jax.ShapeDtypeStruct((m2, 2 * D_MODEL), F32),
                   jax.ShapeDtypeStruct((m2, 6 * NA_WIDTH), BF16),
                   jax.ShapeDtypeStruct((2, m2, 3 * HY_WIDTH), F32)],
        compiler_params=pltpu.CompilerParams(dimension_semantics=("parallel", "parallel"),
                                             vmem_limit_bytes=VMEM_LIMIT),
        name="rows_in",
    )(x, g_ffa, wg, wu, wd, g_mix, w_in)


def _rows_out_kernel(x_ref, yna_ref, yhy_ref, p_ref, g_out_ref, w_out_ref, g_ffb_ref, wg_ref, wu_ref, wd_ref,
                     g_ple_ref, w_pg_ref, w_pp_ref, g_fin_ref, o_ref, *, final):
    g_out = g_out_ref[...]
    y = jnp.concatenate([(yna_ref[...] * g_out[:, :NA_WIDTH]).astype(BF16),
                         (_rms(yhy_ref[...]) * g_out[:, NA_WIDTH:]).astype(BF16)], axis=-1)
    x = x_ref[...] + _dot(y, w_out_ref[...])
    x = _swiglu_residual(x, g_ffb_ref[...], wg_ref, wu_ref, wd_ref)
    h = (_rms(x) * g_ple_ref[...]).astype(BF16)
    gate = jax.nn.sigmoid(_dot(h, w_pg_ref[...]))
    x = x + gate * _dot(p_ref[...].astype(BF16), w_pp_ref[...])
    if final:
        x = _rms(x) * g_fin_ref[...]
    o_ref[...] = x


def _rows_out(x, yna, yhy, p, layer, g_out, w_out, g_ffb, wg, wu, wd, g_ple, w_pg, w_pp, g_fin, final):
    m2 = x.shape[0]
    stacks = (g_out, w_out, g_ffb, wg, wu, wd, g_ple, w_pg, w_pp)
    return pl.pallas_call(
        functools.partial(_rows_out_kernel, final=final),
        grid=(m2 // ROW_TILE, 2),
        in_specs=[_pair_spec(D_MODEL), _pair_spec(NA_WIDTH), _split_spec(HY_WIDTH),
                  pl.BlockSpec((None, ROW_TILE, PLE_DIM), lambda i, par: (layer, i, par))]
                 + [_layer_spec(c, layer) for c in stacks] + [_const_spec(g_fin.shape)],
        out_specs=_pair_spec(D_MODEL),
        out_shape=jax.ShapeDtypeStruct((m2, 2 * D_MODEL), F32),
        compiler_params=pltpu.CompilerParams(dimension_semantics=("parallel", "parallel"),
                                             vmem_limit_bytes=VMEM_LIMIT),
        name="rows_out",
    )(x, yna, yhy, p, *stacks, g_fin)


def _na_key_base(g, rows):
    return jnp.clip(NA_QROWS * g - NA_KH // 2, 0, rows - NA_KROWS)


def _na_bias_tables(rpb, rows):
    n_groups = rows // NA_QROWS
    tables = []
    qc = np.arange(GRID_W)
    cs = np.clip(qc - NA_KW // 2, 0, GRID_W - NA_KW)
    kc = np.arange(GRID_W)
    col_ok = (kc[None, :] >= cs[:, None]) & (kc[None, :] < cs[:, None] + NA_KW)
    dc_idx = np.clip(kc[None, :] - qc[:, None] + NA_KW - 1, 0, 2 * NA_KW - 2)
    dc_hot = jnp.asarray(np.eye(2 * NA_KW - 1, dtype=np.float32)[dc_idx])
    blocks = jnp.einsum("hab,qcb->haqc", rpb, dc_hot, precision=lax.Precision.HIGHEST)
    blocks = jnp.where(jnp.asarray(col_ok)[None, None], blocks, -jnp.inf)
    hidden = jnp.full((NA_HEADS, GRID_W, GRID_W), -jnp.inf, F32)
    for g in (0, 1, n_groups - 1):
        r = NA_QROWS * g + np.arange(NA_QROWS)
        base = int(np.clip(NA_QROWS * g - NA_KH // 2, 0, rows - NA_KROWS))
        krow = base + np.arange(NA_KROWS)
        rs = np.clip(r - NA_KH // 2, 0, rows - NA_KH)
        row_ok = (krow[None, :] >= rs[:, None]) & (krow[None, :] < rs[:, None] + NA_KH)
        dr_idx = krow[None, :] - r[:, None] + NA_KH - 1
        strips = [jnp.concatenate([blocks[:, dr_idx[q, k]] if row_ok[q, k] else hidden for k in range(NA_KROWS)],
                                  axis=-1) for q in range(NA_QROWS)]
        tables.append(jnp.concatenate(strips, axis=1))
    return jnp.stack(tables)


def _na_kernel(q_ref, k_ref, v_ref, bias_ref, o_ref, *, rows):
    g = pl.program_id(1)
    start = pl.multiple_of(_na_key_base(g, rows) * GRID_W, GRID_W)
    nk = NA_KROWS * GRID_W
    lane = lax.broadcasted_iota(jnp.int32, (1, LANES), 1)
    outs = []
    for pair in range(NA_HEADS // 2):
        cols = slice(pair * LANES, (pair + 1) * LANES)
        qp = q_ref[0, :, cols]
        kp = k_ref[0, pl.ds(start, nk), cols]
        vp = v_ref[0, pl.ds(start, nk), cols]
        acc = None
        for sub in range(2):
            mine = (lane < NA_HEAD_DIM) if sub == 0 else (lane >= NA_HEAD_DIM)
            qm = jnp.where(mine, qp, 0) * jnp.asarray(NA_HEAD_DIM ** -0.5, BF16)
            s = lax.dot_general(qm, kp, (((1,), (1,)), ((), ())), preferred_element_type=F32)
            s = s + bias_ref[0, 2 * pair + sub]
            e = jnp.exp(s - jnp.max(s, axis=-1, keepdims=True))
            denom = jnp.sum(e, axis=-1, keepdims=True)
            o = _dot(e.astype(BF16), jnp.where(mine, vp, 0)) * (1.0 / denom)
            acc = o if acc is None else acc + o
        outs.append(acc)
    o_ref[0] = _rms(jnp.concatenate(outs, axis=-1))


def _na(qkv, bias, batch, seq):
    rows = seq // GRID_W
    n_groups = rows // NA_QROWS
    tq = NA_QROWS * GRID_W
    tk = NA_KROWS * GRID_W

    def pattern(b, g):
        return (jnp.where(g == 0, 0, jnp.where(g == n_groups - 1, 2, 1)), 0, 0, 0)

    return pl.pallas_call(
        functools.partial(_na_kernel, rows=rows),
        grid=(batch, n_groups),
        in_specs=[pl.BlockSpec((1, tq, NA_WIDTH), lambda b, g: (b, g, 0)),
                  pl.BlockSpec((1, seq, NA_WIDTH), lambda b, g: (b, 0, 1)),
                  pl.BlockSpec((1, seq, NA_WIDTH), lambda b, g: (b, 0, 2)),
                  pl.BlockSpec((1, NA_HEADS, tq, tk), pattern)],
        out_specs=pl.BlockSpec((1, tq, NA_WIDTH), lambda b, g: (b, g, 0)),
        out_shape=jax.ShapeDtypeStruct((batch, seq, NA_WIDTH), F32),
        compiler_params=pltpu.CompilerParams(dimension_semantics=("parallel", "arbitrary"),
                                             vmem_limit_bytes=VMEM_LIMIT),
        name="na",
    )(qkv, qkv, qkv, bias)


def _dft_mats(seq):
    half = seq // 2
    m = jnp.arange(half, dtype=jnp.int32)[None, :]
    a = jnp.arange(half // DFT_SPLIT, dtype=jnp.int32)[:, None]
    b = jnp.arange(DFT_SPLIT, dtype=jnp.int32)[:, None]
    ang_a = ((DFT_SPLIT * a * m) % seq).astype(F32) * (2.0 * math.pi / seq)
    ang_b = ((b * m) % seq).astype(F32) * (2.0 * math.pi / seq)
    ca, sa = jnp.cos(ang_a)[:, None, :], jnp.sin(ang_a)[:, None, :]
    cb, sb = jnp.cos(ang_b)[None, :, :], jnp.sin(ang_b)[None, :, :]
    cmat = (ca * cb - sa * sb).reshape(half, half)
    smat = (sa * cb + ca * sb).reshape(half, half)
    ang_t = jnp.arange(half, dtype=F32)[:, None] * (math.pi / seq)
    return cmat.astype(BF16), smat.astype(BF16), jnp.cos(ang_t), jnp.sin(ang_t)


def _filter_features(seq):
    bands = (FILTER_EMB - 1) // 2
    t = jnp.linspace(0.0, 1.0, seq, dtype=F32)[:, None]
    w = 2.0 * math.pi * jnp.arange(seq, dtype=F32)[:, None] / seq
    f = jnp.linspace(1e-4, bands - 1, bands, dtype=F32)[None, :]
    z = jnp.concatenate([t, jnp.cos(f * w), -jnp.sin(f * w)], axis=-1)
    return t, jnp.pad(z, ((0, 0), (0, LANES - FILTER_EMB)))


def _half_spectra(xe, xo, c_ref, s_ref, ct, st):
    nc = xe.shape[1]
    x = jnp.concatenate([xe, xo], axis=-1)
    rc = _dot(c_ref[...], x)
    rs = _dot(s_ref[...], x)
    ae, ao = rc[:, :nc], rc[:, nc:]
    be, bo = rs[:, :nc], rs[:, nc:]
    tr = ct * ao - st * bo
    ti = ct * bo + st * ao
    return ae + tr, -(be + ti), ae - tr, ti - be


def _alternating(n):
    row = lax.broadcasted_iota(jnp.int32, (n, 1), 0)
    return jnp.where((row & 1) == 0, 1.0, -1.0)


def _filt_kernel(ze_ref, zo_ref, w1_ref, b1_ref, w2_ref, b2_ref, w3_ref, b3_ref, w4_ref, fr_ref, te_ref, to_ref,
                 dl_ref, c_ref, s_ref, ct_ref, st_ref, kpr_ref, kpi_ref, kqr_ref, kqi_ref, kmid_ref):
    half = ze_ref.shape[0]
    n_fft = 4 * half
    hp = lax.Precision.HIGHEST
    fr = fr_ref[...]

    def taps(z_ref, t_ref):
        h = jnp.sin(fr * (jnp.dot(z_ref[...], w1_ref[...], precision=hp, preferred_element_type=F32) + b1_ref[...]))
        h = jnp.sin(fr * (jnp.dot(h, w2_ref[...], precision=hp, preferred_element_type=F32) + b2_ref[...]))
        h = jnp.sin(fr * (jnp.dot(h, w3_ref[...], precision=hp, preferred_element_type=F32) + b3_ref[...]))
        h = jnp.dot(h, w4_ref[...], precision=hp, preferred_element_type=F32)
        decay = jnp.exp(-t_ref[...] * dl_ref[...])
        return h * jnp.concatenate([decay, decay], axis=-1)

    he = taps(ze_ref, te_ref)
    ho = taps(zo_ref, to_ref)
    first = lax.broadcasted_iota(jnp.int32, (half, 1), 0) == 0
    hf0, hb0 = he[:, :HY_WIDTH], he[:, HY_WIDTH:]
    tot = jnp.sum(jnp.abs(he) + jnp.abs(ho), axis=0, keepdims=True)
    lag0 = jnp.sum(jnp.where(first, jnp.abs(hf0 + hb0) - jnp.abs(hf0) - jnp.abs(hb0), 0.0), axis=0, keepdims=True)
    inv = 1.0 / (tot[:, :HY_WIDTH] + tot[:, HY_WIDTH:] + lag0)

    ct, st = ct_ref[...], st_ref[...]
    fpr, fpi, fqr, fqi = _half_spectra(he[:, :HY_WIDTH].astype(BF16), ho[:, :HY_WIDTH].astype(BF16),
                                       c_ref, s_ref, ct, st)
    bpr, bpi, bqr, bqi = _half_spectra(he[:, HY_WIDTH:].astype(BF16), ho[:, HY_WIDTH:].astype(BF16),
                                       c_ref, s_ref, ct, st)
    scale = inv * jnp.where(first, 1.0 / n_fft, 2.0 / n_fft)
    kpr_ref[...] = (fpr + bpr) * scale
    kpi_ref[...] = (fpi - bpi) * scale
    kqr_ref[...] = (fqr + bqr) * scale
    kqi_ref[...] = (fqi - bqi) * scale
    alt = _alternating(half)
    a1 = jnp.sum(he * alt, axis=0, keepdims=True)
    b1 = jnp.sum(ho * alt, axis=0, keepdims=True)
    mid = inv * (2.0 / n_fft)
    kmid_ref[0:1, :] = (a1[:, :HY_WIDTH] + a1[:, HY_WIDTH:]) * mid
    kmid_ref[1:2, :] = (b1[:, HY_WIDTH:] - b1[:, :HY_WIDTH]) * mid


def _filter_spectrum(z, t, deltas, cmat, smat, ct, st, w1, b1, w2, b2, w3, b3, w4, freq):
    half = z.shape[0] // 2
    pad2 = lambda w, r, c: jnp.pad(w, ((0, r - w.shape[0]), (0, c - w.shape[1])))
    args = (z[0::2], z[1::2], pad2(w1, LANES, LANES), pad2(b1[None], 1, LANES), pad2(w2, LANES, LANES),
            pad2(b2[None], 1, LANES), pad2(w3, LANES, LANES), pad2(b3[None], 1, LANES),
            pad2(w4, LANES, 2 * HY_WIDTH), pad2(freq[None], 1, LANES), t[0::2], t[1::2], deltas, cmat, smat, ct, st)
    spec = jax.ShapeDtypeStruct((half, HY_WIDTH), F32)
    return pl.pallas_call(
        _filt_kernel,
        out_shape=[spec, spec, spec, spec, jax.ShapeDtypeStruct((2, HY_WIDTH), F32)],
        compiler_params=pltpu.CompilerParams(vmem_limit_bytes=VMEM_LIMIT),
        name="filt",
    )(*args)


def _short_conv(u_ref, w_ref, b_ref):
    half = u_ref.shape[2]
    row = lax.broadcasted_iota(jnp.int32, (half, 1), 0)
    ue = u_ref[0, 0]
    uo = u_ref[1, 0]
    uo_prev = jnp.where(row == 0, 0.0, pltpu.roll(uo, 1, 0))
    ue_next = jnp.where(row == half - 1, 0.0, pltpu.roll(ue, half - 1, 0))
    w0, w1, w2, b = w_ref[0:1, :], w_ref[1:2, :], w_ref[2:3, :], b_ref[...]
    return uo_prev * w0 + ue * w1 + uo * w2 + b, ue * w0 + uo * w1 + ue_next * w2 + b


def _hyena_kernel(x0_ref, x1_ref, hv_ref, w0_ref, w1_ref, wv_ref, b0_ref, b1_ref, bv_ref,
                  c_ref, s_ref, ct_ref, st_ref, kpr_ref, kpi_ref, kqr_ref, kqi_ref, kmid_ref, hb_ref, o_ref):
    half = x0_ref.shape[2]
    nc = x0_ref.shape[3]
    ct, st = ct_ref[...], st_ref[...]
    hve, hvo = _short_conv(hv_ref, wv_ref, bv_ref)
    x1e, x1o = _short_conv(x1_ref, w1_ref, b1_ref)
    ve, vo = hve * x1e, hvo * x1o
    pr, pi, qr, qi = _half_spectra(ve.astype(BF16), vo.astype(BF16), c_ref, s_ref, ct, st)
    kpr, kpi, kqr, kqi = kpr_ref[...], kpi_ref[...], kqr_ref[...], kqi_ref[...]
    ypr, ypi = pr * kpr - pi * kpi, pr * kpi + pi * kpr
    yqr, yqi = qr * kqr - qi * kqi, qr * kqi + qi * kqr
    dr, di = ypr - yqr, ypi - yqi
    zr = jnp.concatenate([(ypr + yqr).astype(BF16), (dr * ct - di * st).astype(BF16)], axis=-1)
    zi = jnp.concatenate([(ypi + yqi).astype(BF16), (dr * st + di * ct).astype(BF16)], axis=-1)
    y = _dot(c_ref[...], zr) - _dot(s_ref[...], zi)
    alt = _alternating(half)
    a1 = jnp.sum(ve * alt, axis=0, keepdims=True)
    b1 = jnp.sum(vo * alt, axis=0, keepdims=True)
    kmr, kmi = kmid_ref[0:1, :], kmid_ref[1:2, :]
    ye = y[:, :nc] + alt * (a1 * kmr + b1 * kmi)
    yo = y[:, nc:] - alt * (a1 * kmi - b1 * kmr)
    x0e, x0o = _short_conv(x0_ref, w0_ref, b0_ref)
    hb = hb_ref[...]
    o_ref[0, 0] = x0e * (ye + ve * hb)
    o_ref[1, 0] = x0o * (yo + vo * hb)


def _hyena(hy, w_sc, b_sc, cmat, smat, ct, st, kspec, hy_bias, batch, seq):
    kpr, kpi, kqr, kqi, kmid = kspec
    nt = HY_WIDTH // HY_CTILE
    half = seq // 2
    blk = lambda off: pl.BlockSpec((2, 1, half, HY_CTILE), lambda j, b: (0, b, 0, off * nt + j))
    wblk = lambda off: pl.BlockSpec((3, HY_CTILE), lambda j, b: (0, off * nt + j))
    bblk = lambda off: pl.BlockSpec((1, HY_CTILE), lambda j, b: (0, off * nt + j))
    chan = lambda r: pl.BlockSpec((r, HY_CTILE), lambda j, b: (0, j))
    spec = pl.BlockSpec((half, HY_CTILE), lambda j, b: (0, j), pipeline_mode=pl.Buffered(1))
    return pl.pallas_call(
        _hyena_kernel,
        grid=(nt, batch),
        in_specs=[blk(0), blk(1), blk(2), wblk(0), wblk(1), wblk(2), bblk(0), bblk(1), bblk(2),
                  _const_spec(cmat.shape), _const_spec(smat.shape), _const_spec(ct.shape), _const_spec(st.shape),
                  spec, spec, spec, spec, chan(2), chan(1)],
        out_specs=pl.BlockSpec((2, 1, half, HY_CTILE), lambda j, b: (0, b, 0, j)),
        out_shape=jax.ShapeDtypeStruct((2, batch, half, HY_WIDTH), F32),
        compiler_params=pltpu.CompilerParams(dimension_semantics=("arbitrary", "arbitrary"),
                                             vmem_limit_bytes=VMEM_LIMIT),
        name="hyena",
    )(hy, hy, hy, w_sc, w_sc, w_sc, b_sc, b_sc, b_sc, cmat, smat, ct, st, kpr, kpi, kqr, kqi, kmid, hy_bias)


def kernel(x, p, g_ffa, w_ffa_gate, w_ffa_up, w_ffa_down, g_mix, w_in, na_rpb, w_sc, b_sc, w_f1, b_f1, w_f2, b_f2, w_f3, b_f3, w_f4, filt_freq, hy_bias, g_out, w_out, g_ffb, w_ffb_gate, w_ffb_up, w_ffb_down, g_ple, w_ple_gate, w_ple_proj, g_final):
    batch, seq, _ = x.shape
    depth = p.shape[0]
    m = batch * seq
    rows = seq // GRID_W
    bf = lambda w: w.astype(BF16)
    vec = lambda g: g[:, None, :]

    cmat, smat, ct, st = _dft_mats(seq)
    t, z = _filter_features(seq)
    deltas = jnp.abs(jnp.linspace(MIN_DECAY, MAX_DECAY, HY_WIDTH, dtype=F32))[None, :]

    in_params = (vec(g_ffa), bf(w_ffa_gate), bf(w_ffa_up), bf(w_ffa_down), vec(g_mix), bf(w_in))
    out_params = (vec(g_out), bf(w_out), vec(g_ffb), bf(w_ffb_gate), bf(w_ffb_up), bf(w_ffb_down),
                  vec(g_ple), bf(w_ple_gate), bf(w_ple_proj))
    m2 = m // 2
    pr = p.reshape(depth, m2, 2 * PLE_DIM)

    xr = x.reshape(m2, 2 * D_MODEL)
    for i in range(depth):
        xr, qkv, hy = _rows_in(xr, i, *in_params)
        bias = _na_bias_tables(na_rpb[i], rows)
        yna = _na(qkv.reshape(batch, seq, 3 * NA_WIDTH), bias, batch, seq)
        kspec = _filter_spectrum(z, t, deltas, cmat, smat, ct, st, w_f1[i], b_f1[i], w_f2[i], b_f2[i],
                                 w_f3[i], b_f3[i], w_f4[i], filt_freq[i])
        yhy = _hyena(hy.reshape(2, batch, seq // 2, 3 * HY_WIDTH), w_sc[i], b_sc[i][None], cmat, smat, ct, st, kspec,
                     hy_bias[i][None], batch, seq)
        xr = _rows_out(xr, yna.reshape(m2, 2 * NA_WIDTH), yhy.reshape(2, m2, HY_WIDTH), pr, i, *out_params,
                       g_final[None], final=(i == depth - 1))
    return xr.reshape(batch, seq, D_MODEL)
```

```python
import functools
import math

import numpy as np
import jax
import jax.numpy as jnp
from jax import lax
from jax.experimental import pallas as pl
from jax.experimental.pallas import tpu as pltpu

F32 = jnp.float32
BF16 = jnp.bfloat16

D_MODEL = 1024
GRID_W = 64
NA_HEADS = 8
NA_HEAD_DIM = 64
NA_WIDTH = NA_HEADS * NA_HEAD_DIM
NA_KH = 8
NA_KW = 16
HY_WIDTH = D_MODEL - NA_WIDTH
FILTER_EMB = 33
DECAY_TARGET = 1e-2
FAST_DECAY_PCT = 0.3
SLOW_DECAY_PCT = 1.5
MAX_DECAY = math.log(DECAY_TARGET) / FAST_DECAY_PCT
MIN_DECAY = math.log(DECAY_TARGET) / SLOW_DECAY_PCT
D_FF = 2816
PLE_DIM = 256
EPS = 1e-6

LANES = 128
ROW_TILE = 256
FF_CHUNKS = ((0, 1024), (1024, 2048), (2048, D_FF))
NA_QROWS = 4
NA_KROWS = 12
HY_CTILE = 128
DFT_SPLIT = 64
VMEM_LIMIT = 56 * 1024 * 1024


def _rms(x):
    return x * lax.rsqrt(jnp.mean(x * x, axis=-1, keepdims=True) + EPS)


def _dot(a, b):
    return jnp.dot(a, b, preferred_element_type=F32)


def _swiglu_residual(x, g, wg_ref, wu_ref, wd_ref):
    h = (_rms(x) * g).astype(BF16)
    acc = None
    for lo, hi in FF_CHUNKS:
        gate = _dot(h, wg_ref[:, lo:hi])
        up = _dot(h, wu_ref[:, lo:hi])
        act = (gate * jax.nn.sigmoid(gate) * up).astype(BF16)
        part = _dot(act, wd_ref[lo:hi, :])
        acc = part if acc is None else acc + part
    return x + 0.5 * acc


def _const_spec(shape):
    nd = len(shape)
    return pl.BlockSpec(shape, lambda *_: (0,) * nd, pipeline_mode=pl.Buffered(1))


def _layer_spec(stacked, layer):
    return pl.BlockSpec((None,) + stacked.shape[1:], lambda *_: (layer, 0, 0), pipeline_mode=pl.Buffered(1))


def _row_spec(width):
    return pl.BlockSpec((ROW_TILE, width), lambda i: (i, 0))


def _rows_in_kernel(x_ref, g_ffa_ref, wg_ref, wu_ref, wd_ref, g_mix_ref, w_in_ref,
                    x_out_ref, qkv_ref, hy_ref):
    x = _swiglu_residual(x_ref[...], g_ffa_ref[...], wg_ref, wu_ref, wd_ref)
    x_out_ref[...] = x
    h = (_rms(x) * g_mix_ref[...]).astype(BF16)
    nq = 3 * NA_WIDTH
    qkv_ref[...] = _dot(h, w_in_ref[:, :nq]).astype(BF16)
    hy_ref[...] = _dot(h, w_in_ref[:, nq:])


def _rows_in(x, layer, g_ffa, wg, wu, wd, g_mix, w_in):
    m = x.shape[0]
    return pl.pallas_call(
        _rows_in_kernel,
        grid=(m // ROW_TILE,),
        in_specs=[_row_spec(D_MODEL)] + [_layer_spec(c, layer) for c in (g_ffa, wg, wu, wd, g_mix, w_in)],
        out_specs=[_row_spec(D_MODEL), _row_spec(3 * NA_WIDTH), _row_spec(3 * HY_WIDTH)],
        out_shape=[jax.ShapeDtypeStruct((m, D_MODEL), F32),
                   jax.ShapeDtypeStruct((m, 3 * NA_WIDTH), BF16),
                   jax.ShapeDtypeStruct((m, 3 * HY_WIDTH), F32)],
        compiler_params=pltpu.CompilerParams(dimension_semantics=("parallel",), vmem_limit_bytes=VMEM_LIMIT),
        name="rows_in",
    )(x, g_ffa, wg, wu, wd, g_mix, w_in)


def _rows_out_kernel(x_ref, yna_ref, yhy_ref, p_ref, g_out_ref, w_out_ref, g_ffb_ref, wg_ref, wu_ref, wd_ref,
                     g_ple_ref, w_pg_ref, w_pp_ref, g_fin_ref, o_ref, *, final):
    g_out = g_out_ref[...]
    y = jnp.concatenate([(yna_ref[...] * g_out[:, :NA_WIDTH]).astype(BF16),
                         (_rms(yhy_ref[...]) * g_out[:, NA_WIDTH:]).astype(BF16)], axis=-1)
    x = x_ref[...] + _dot(y, w_out_ref[...])
    x = _swiglu_residual(x, g_ffb_ref[...], wg_ref, wu_ref, wd_ref)
    h = (_rms(x) * g_ple_ref[...]).astype(BF16)
    gate = jax.nn.sigmoid(_dot(h, w_pg_ref[...]))
    x = x + gate * _dot(p_ref[...].astype(BF16), w_pp_ref[...])
    if final:
        x = _rms(x) * g_fin_ref[...]
    o_ref[...] = x


def _rows_out(x, yna, yhy, p, layer, g_out, w_out, g_ffb, wg, wu, wd, g_ple, w_pg, w_pp, g_fin, final):
    m = x.shape[0]
    stacks = (g_out, w_out, g_ffb, wg, wu, wd, g_ple, w_pg, w_pp)
    return pl.pallas_call(
        functools.partial(_rows_out_kernel, final=final),
        grid=(m // ROW_TILE,),
        in_specs=[_row_spec(D_MODEL), _row_spec(NA_WIDTH), _row_spec(HY_WIDTH),
                  pl.BlockSpec((None, ROW_TILE, PLE_DIM), lambda i: (layer, i, 0))]
                 + [_layer_spec(c, layer) for c in stacks] + [_const_spec(g_fin.shape)],
        out_specs=_row_spec(D_MODEL),
        out_shape=jax.ShapeDtypeStruct((m, D_MODEL), F32),
        compiler_params=pltpu.CompilerParams(dimension_semantics=("parallel",), vmem_limit_bytes=VMEM_LIMIT),
        name="rows_out",
    )(x, yna, yhy, p, *stacks, g_fin)


def _na_key_base(g, rows):
    return jnp.clip(NA_QROWS * g - NA_KH // 2, 0, rows - NA_KROWS)


def _na_bias_tables(rpb, rows):
    n_groups = rows // NA_QROWS
    tables = []
    qc = np.arange(GRID_W)
    cs = np.clip(qc - NA_KW // 2, 0, GRID_W - NA_KW)
    kc = np.arange(GRID_W)
    col_ok = (kc[None, :] >= cs[:, None]) & (kc[None, :] < cs[:, None] + NA_KW)
    dc_idx = np.clip(kc[None, :] - qc[:, None] + NA_KW - 1, 0, 2 * NA_KW - 2)
    dc_hot = jnp.asarray(np.eye(2 * NA_KW - 1, dtype=np.float32)[dc_idx])
    blocks = jnp.einsum("hab,qcb->haqc", rpb, dc_hot, precision=lax.Precision.HIGHEST)
    blocks = jnp.where(jnp.asarray(col_ok)[None, None], blocks, -jnp.inf)
    hidden = jnp.full((NA_HEADS, GRID_W, GRID_W), -jnp.inf, F32)
    for g in (0, 1, n_groups - 1):
        r = NA_QROWS * g + np.arange(NA_QROWS)
        base = int(np.clip(NA_QROWS * g - NA_KH // 2, 0, rows - NA_KROWS))
        krow = base + np.arange(NA_KROWS)
        rs = np.clip(r - NA_KH // 2, 0, rows - NA_KH)
        row_ok = (krow[None, :] >= rs[:, None]) & (krow[None, :] < rs[:, None] + NA_KH)
        dr_idx = krow[None, :] - r[:, None] + NA_KH - 1
        strips = [jnp.concatenate([blocks[:, dr_idx[q, k]] if row_ok[q, k] else hidden for k in range(NA_KROWS)],
                                  axis=-1) for q in range(NA_QROWS)]
        tables.append(jnp.concatenate(strips, axis=1))
    return jnp.stack(tables)


def _na_kernel(q_ref, k_ref, v_ref, bias_ref, o_ref, *, rows):
    g = pl.program_id(1)
    start = pl.multiple_of(_na_key_base(g, rows) * GRID_W, GRID_W)
    nk = NA_KROWS * GRID_W
    lane = lax.broadcasted_iota(jnp.int32, (1, LANES), 1)
    outs = []
    for pair in range(NA_HEADS // 2):
        cols = slice(pair * LANES, (pair + 1) * LANES)
        qp = q_ref[0, :, cols]
        kp = k_ref[0, pl.ds(start, nk), cols]
        vp = v_ref[0, pl.ds(start, nk), cols]
        acc = None
        for sub in range(2):
            mine = (lane < NA_HEAD_DIM) if sub == 0 else (lane >= NA_HEAD_DIM)
            qm = jnp.where(mine, qp, 0) * jnp.asarray(NA_HEAD_DIM ** -0.5, BF16)
            s = lax.dot_general(qm, kp, (((1,), (1,)), ((), ())), preferred_element_type=F32)
            s = s + bias_ref[0, 2 * pair + sub]
            e = jnp.exp(s - jnp.max(s, axis=-1, keepdims=True))
            denom = jnp.sum(e, axis=-1, keepdims=True)
            o = _dot(e.astype(BF16), jnp.where(mine, vp, 0)) * (1.0 / denom)
            acc = o if acc is None else acc + o
        outs.append(acc)
    o_ref[0] = _rms(jnp.concatenate(outs, axis=-1))


def _na(qkv, bias, batch, seq):
    rows = seq // GRID_W
    n_groups = rows // NA_QROWS
    tq = NA_QROWS * GRID_W
    tk = NA_KROWS * GRID_W

    def pattern(b, g):
        return (jnp.where(g == 0, 0, jnp.where(g == n_groups - 1, 2, 1)), 0, 0, 0)

    return pl.pallas_call(
        functools.partial(_na_kernel, rows=rows),
        grid=(batch, n_groups),
        in_specs=[pl.BlockSpec((1, tq, NA_WIDTH), lambda b, g: (b, g, 0)),
                  pl.BlockSpec((1, seq, NA_WIDTH), lambda b, g: (b, 0, 1)),
                  pl.BlockSpec((1, seq, NA_WIDTH), lambda b, g: (b, 0, 2)),
                  pl.BlockSpec((1, NA_HEADS, tq, tk), pattern)],
        out_specs=pl.BlockSpec((1, tq, NA_WIDTH), lambda b, g: (b, g, 0)),
        out_shape=jax.ShapeDtypeStruct((batch, seq, NA_WIDTH), F32),
        compiler_params=pltpu.CompilerParams(dimension_semantics=("parallel", "arbitrary"),
                                             vmem_limit_bytes=VMEM_LIMIT),
        name="na",
    )(qkv, qkv, qkv, bias)


def _dft_mats(seq):
    half = seq // 2
    m = jnp.arange(half, dtype=jnp.int32)[None, :]
    a = jnp.arange(half // DFT_SPLIT, dtype=jnp.int32)[:, None]
    b = jnp.arange(DFT_SPLIT, dtype=jnp.int32)[:, None]
    ang_a = ((DFT_SPLIT * a * m) % seq).astype(F32) * (2.0 * math.pi / seq)
    ang_b = ((b * m) % seq).astype(F32) * (2.0 * math.pi / seq)
    ca, sa = jnp.cos(ang_a)[:, None, :], jnp.sin(ang_a)[:, None, :]
    cb, sb = jnp.cos(ang_b)[None, :, :], jnp.sin(ang_b)[None, :, :]
    cmat = (ca * cb - sa * sb).reshape(half, half)
    smat = (sa * cb + ca * sb).reshape(half, half)
    ang_t = jnp.arange(half, dtype=F32)[:, None] * (math.pi / seq)
    return cmat.astype(BF16), smat.astype(BF16), jnp.cos(ang_t), jnp.sin(ang_t)


def _filter_features(seq):
    bands = (FILTER_EMB - 1) // 2
    t = jnp.linspace(0.0, 1.0, seq, dtype=F32)[:, None]
    w = 2.0 * math.pi * jnp.arange(seq, dtype=F32)[:, None] / seq
    f = jnp.linspace(1e-4, bands - 1, bands, dtype=F32)[None, :]
    z = jnp.concatenate([t, jnp.cos(f * w), -jnp.sin(f * w)], axis=-1)
    return t, jnp.pad(z, ((0, 0), (0, LANES - FILTER_EMB)))


def _half_spectra(xe, xo, c_ref, s_ref, ct, st):
    nc = xe.shape[1]
    x = jnp.concatenate([xe, xo], axis=-1)
    rc = _dot(c_ref[...], x)
    rs = _dot(s_ref[...], x)
    ae, ao = rc[:, :nc], rc[:, nc:]
    be, bo = rs[:, :nc], rs[:, nc:]
    tr = ct * ao - st * bo
    ti = ct * bo + st * ao
    return ae + tr, -(be + ti), ae - tr, ti - be


def _alternating(n):
    row = lax.broadcasted_iota(jnp.int32, (n, 1), 0)
    return jnp.where((row & 1) == 0, 1.0, -1.0)


def _filt_kernel(ze_ref, zo_ref, w1_ref, b1_ref, w2_ref, b2_ref, w3_ref, b3_ref, w4_ref, fr_ref, te_ref, to_ref,
                 dl_ref, c_ref, s_ref, ct_ref, st_ref, kpr_ref, kpi_ref, kqr_ref, kqi_ref, kmid_ref):
    half = ze_ref.shape[0]
    n_fft = 4 * half
    hp = lax.Precision.HIGHEST
    fr = fr_ref[...]

    def taps(z_ref, t_ref):
        h = jnp.sin(fr * (jnp.dot(z_ref[...], w1_ref[...], precision=hp, preferred_element_type=F32) + b1_ref[...]))
        h = jnp.sin(fr * (jnp.dot(h, w2_ref[...], precision=hp, preferred_element_type=F32) + b2_ref[...]))
        h = jnp.sin(fr * (jnp.dot(h, w3_ref[...], precision=hp, preferred_element_type=F32) + b3_ref[...]))
        h = jnp.dot(h, w4_ref[...], precision=hp, preferred_element_type=F32)
        decay = jnp.exp(-t_ref[...] * dl_ref[...])
        return h * jnp.concatenate([decay, decay], axis=-1)

    he = taps(ze_ref, te_ref)
    ho = taps(zo_ref, to_ref)
    first = lax.broadcasted_iota(jnp.int32, (half, 1), 0) == 0
    hf0, hb0 = he[:, :HY_WIDTH], he[:, HY_WIDTH:]
    tot = jnp.sum(jnp.abs(he) + jnp.abs(ho), axis=0, keepdims=True)
    lag0 = jnp.sum(jnp.where(first, jnp.abs(hf0 + hb0) - jnp.abs(hf0) - jnp.abs(hb0), 0.0), axis=0, keepdims=True)
    inv = 1.0 / (tot[:, :HY_WIDTH] + tot[:, HY_WIDTH:] + lag0)

    ct, st = ct_ref[...], st_ref[...]
    fpr, fpi, fqr, fqi = _half_spectra(he[:, :HY_WIDTH].astype(BF16), ho[:, :HY_WIDTH].astype(BF16),
                                       c_ref, s_ref, ct, st)
    bpr, bpi, bqr, bqi = _half_spectra(he[:, HY_WIDTH:].astype(BF16), ho[:, HY_WIDTH:].astype(BF16),
                                       c_ref, s_ref, ct, st)
    scale = inv * jnp.where(first, 1.0 / n_fft, 2.0 / n_fft)
    kpr_ref[...] = (fpr + bpr) * scale
    kpi_ref[...] = (fpi - bpi) * scale
    kqr_ref[...] = (fqr + bqr) * scale
    kqi_ref[...] = (fqi - bqi) * scale
    alt = _alternating(half)
    a1 = jnp.sum(he * alt, axis=0, keepdims=True)
    b1 = jnp.sum(ho * alt, axis=0, keepdims=True)
    mid = inv * (2.0 / n_fft)
    kmid_ref[0:1, :] = (a1[:, :HY_WIDTH] + a1[:, HY_WIDTH:]) * mid
    kmid_ref[1:2, :] = (b1[:, HY_WIDTH:] - b1[:, :HY_WIDTH]) * mid


def _filter_spectrum(z, t, deltas, cmat, smat, ct, st, w1, b1, w2, b2, w3, b3, w4, freq):
    half = z.shape[0] // 2
    pad2 = lambda w, r, c: jnp.pad(w, ((0, r - w.shape[0]), (0, c - w.shape[1])))
    args = (z[0::2], z[1::2], pad2(w1, LANES, LANES), pad2(b1[None], 1, LANES), pad2(w2, LANES, LANES),
            pad2(b2[None], 1, LANES), pad2(w3, LANES, LANES), pad2(b3[None], 1, LANES),
            pad2(w4, LANES, 2 * HY_WIDTH), pad2(freq[None], 1, LANES), t[0::2], t[1::2], deltas, cmat, smat, ct, st)
    spec = jax.ShapeDtypeStruct((half, HY_WIDTH), F32)
    return pl.pallas_call(
        _filt_kernel,
        out_shape=[spec, spec, spec, spec, jax.ShapeDtypeStruct((2, HY_WIDTH), F32)],
        compiler_params=pltpu.CompilerParams(vmem_limit_bytes=VMEM_LIMIT),
        name="filt",
    )(*args)


def _short_conv(u_ref, w_ref, b_ref):
    half = u_ref.shape[0] // 2
    row = lax.broadcasted_iota(jnp.int32, (half, 1), 0)
    ue = u_ref[pl.ds(0, half, stride=2), :]
    uo = u_ref[pl.ds(1, half, stride=2), :]
    uo_prev = jnp.where(row == 0, 0.0, pltpu.roll(uo, 1, 0))
    ue_next = jnp.where(row == half - 1, 0.0, pltpu.roll(ue, half - 1, 0))
    w0, w1, w2, b = w_ref[0:1, :], w_ref[1:2, :], w_ref[2:3, :], b_ref[...]
    return uo_prev * w0 + ue * w1 + uo * w2 + b, ue * w0 + uo * w1 + ue_next * w2 + b


def _hyena_kernel(x0_ref, x1_ref, hv_ref, w0_ref, w1_ref, wv_ref, b0_ref, b1_ref, bv_ref,
                  c_ref, s_ref, ct_ref, st_ref, kpr_ref, kpi_ref, kqr_ref, kqi_ref, kmid_ref, hb_ref, o_ref):
    half = x0_ref.shape[0] // 2
    nc = x0_ref.shape[1]
    ct, st = ct_ref[...], st_ref[...]
    hve, hvo = _short_conv(hv_ref, wv_ref, bv_ref)
    x1e, x1o = _short_conv(x1_ref, w1_ref, b1_ref)
    ve, vo = hve * x1e, hvo * x1o
    pr, pi, qr, qi = _half_spectra(ve.astype(BF16), vo.astype(BF16), c_ref, s_ref, ct, st)
    kpr, kpi, kqr, kqi = kpr_ref[...], kpi_ref[...], kqr_ref[...], kqi_ref[...]
    ypr, ypi = pr * kpr - pi * kpi, pr * kpi + pi * kpr
    yqr, yqi = qr * kqr - qi * kqi, qr * kqi + qi * kqr
    dr, di = ypr - yqr, ypi - yqi
    zr = jnp.concatenate([(ypr + yqr).astype(BF16), (dr * ct - di * st).astype(BF16)], axis=-1)
    zi = jnp.concatenate([(ypi + yqi).astype(BF16), (dr * st + di * ct).astype(BF16)], axis=-1)
    y = _dot(c_ref[...], zr) - _dot(s_ref[...], zi)
    alt = _alternating(half)
    a1 = jnp.sum(ve * alt, axis=0, keepdims=True)
    b1 = jnp.sum(vo * alt, axis=0, keepdims=True)
    kmr, kmi = kmid_ref[0:1, :], kmid_ref[1:2, :]
    ye = y[:, :nc] + alt * (a1 * kmr + b1 * kmi)
    yo = y[:, nc:] - alt * (a1 * kmi - b1 * kmr)
    x0e, x0o = _short_conv(x0_ref, w0_ref, b0_ref)
    hb = hb_ref[...]
    o_ref[pl.ds(0, half, stride=2), :] = x0e * (ye + ve * hb)
    o_ref[pl.ds(1, half, stride=2), :] = x0o * (yo + vo * hb)


def _hyena(hy, w_sc, b_sc, cmat, smat, ct, st, kspec, hy_bias, batch, seq):
    kpr, kpi, kqr, kqi, kmid = kspec
    nt = HY_WIDTH // HY_CTILE
    half = seq // 2
    blk = lambda off: pl.BlockSpec((None, seq, HY_CTILE), lambda j, b: (b, 0, off * nt + j))
    wblk = lambda off: pl.BlockSpec((3, HY_CTILE), lambda j, b: (0, off * nt + j))
    bblk = lambda off: pl.BlockSpec((1, HY_CTILE), lambda j, b: (0, off * nt + j))
    chan = lambda r: pl.BlockSpec((r, HY_CTILE), lambda j, b: (0, j))
    spec = pl.BlockSpec((half, HY_CTILE), lambda j, b: (0, j), pipeline_mode=pl.Buffered(1))
    return pl.pallas_call(
        _hyena_kernel,
        grid=(nt, batch),
        in_specs=[blk(0), blk(1), blk(2), wblk(0), wblk(1), wblk(2), bblk(0), bblk(1), bblk(2),
                  _const_spec(cmat.shape), _const_spec(smat.shape), _const_spec(ct.shape), _const_spec(st.shape),
                  spec, spec, spec, spec, chan(2), chan(1)],
        out_specs=pl.BlockSpec((None, seq, HY_CTILE), lambda j, b: (b, 0, j)),
        out_shape=jax.ShapeDtypeStruct((batch, seq, HY_WIDTH), F32),
        compiler_params=pltpu.CompilerParams(dimension_semantics=("arbitrary", "arbitrary"),
                                             vmem_limit_bytes=VMEM_LIMIT),
        name="hyena",
    )(hy, hy, hy, w_sc, w_sc, w_sc, b_sc, b_sc, b_sc, cmat, smat, ct, st, kpr, kpi, kqr, kqi, kmid, hy_bias)


def kernel(x, p, g_ffa, w_ffa_gate, w_ffa_up, w_ffa_down, g_mix, w_in, na_rpb, w_sc, b_sc, w_f1, b_f1, w_f2, b_f2, w_f3, b_f3, w_f4, filt_freq, hy_bias, g_out, w_out, g_ffb, w_ffb_gate, w_ffb_up, w_ffb_down, g_ple, w_ple_gate, w_ple_proj, g_final):
    batch, seq, _ = x.shape
    depth = p.shape[0]
    m = batch * seq
    rows = seq // GRID_W
    bf = lambda w: w.astype(BF16)
    vec = lambda g: g[:, None, :]

    cmat, smat, ct, st = _dft_mats(seq)
    t, z = _filter_features(seq)
    deltas = jnp.abs(jnp.linspace(MIN_DECAY, MAX_DECAY, HY_WIDTH, dtype=F32))[None, :]

    in_params = (vec(g_ffa), bf(w_ffa_gate), bf(w_ffa_up), bf(w_ffa_down), vec(g_mix), bf(w_in))
    out_params = (vec(g_out), bf(w_out), vec(g_ffb), bf(w_ffb_gate), bf(w_ffb_up), bf(w_ffb_down),
                  vec(g_ple), bf(w_ple_gate), bf(w_ple_proj))
    pr = p.reshape(depth, m, PLE_DIM)

    xr = x.reshape(m, D_MODEL)
    for i in range(depth):
        xr, qkv, hy = _rows_in(xr, i, *in_params)
        bias = _na_bias_tables(na_rpb[i], rows)
        yna = _na(qkv.reshape(batch, seq, 3 * NA_WIDTH), bias, batch, seq)
        kspec = _filter_spectrum(z, t, deltas, cmat, smat, ct, st, w_f1[i], b_f1[i], w_f2[i], b_f2[i],
                                 w_f3[i], b_f3[i], w_f4[i], filt_freq[i])
        yhy = _hyena(hy.reshape(batch, seq, 3 * HY_WIDTH), w_sc[i], b_sc[i][None], cmat, smat, ct, st, kspec,
                     hy_bias[i][None], batch, seq)
        xr = _rows_out(xr, yna.reshape(m, NA_WIDTH), yhy.reshape(m, HY_WIDTH), pr, i, *out_params,
                       g_final[None], final=(i == depth - 1))
    return xr.reshape(batch, seq, D_MODEL)
```

```python
import functools
import math

import numpy as np
import jax
import jax.numpy as jnp
from jax import lax
from jax.experimental import pallas as pl
from jax.experimental.pallas import tpu as pltpu

F32 = jnp.float32
BF16 = jnp.bfloat16

D_MODEL = 1024
GRID_W = 64
NA_HEADS = 8
NA_HEAD_DIM = 64
NA_WIDTH = NA_HEADS * NA_HEAD_DIM
NA_KH = 8
NA_KW = 16
HY_WIDTH = D_MODEL - NA_WIDTH
FILTER_EMB = 33
DECAY_TARGET = 1e-2
FAST_DECAY_PCT = 0.3
SLOW_DECAY_PCT = 1.5
MAX_DECAY = math.log(DECAY_TARGET) / FAST_DECAY_PCT
MIN_DECAY = math.log(DECAY_TARGET) / SLOW_DECAY_PCT
D_FF = 2816
PLE_DIM = 256
EPS = 1e-6

LANES = 128
ROW_TILE = 512
LOG2E = math.log2(math.e)
NA_QSCALE = NA_HEAD_DIM ** -0.5 * LOG2E
FF_CHUNKS = ((0, 1024), (1024, 2048), (2048, D_FF))
NA_QROWS = 4
NA_KROWS = 12
HY_CTILE = 128
DFT_SPLIT = 64
VMEM_LIMIT = 56 * 1024 * 1024


def _rms(x):
    return x * lax.rsqrt(jnp.mean(x * x, axis=-1, keepdims=True) + EPS)


def _dot(a, b):
    return jnp.dot(a, b, preferred_element_type=F32)


def _swiglu_residual(x, g, wg_ref, wu_ref, wd_ref):
    h = (_rms(x) * g).astype(BF16)
    acc = None
    for lo, hi in FF_CHUNKS:
        gate = _dot(h, wg_ref[:, lo:hi])
        up = _dot(h, wu_ref[:, lo:hi])
        act = (gate * jax.nn.sigmoid(gate) * up).astype(BF16)
        part = _dot(act, wd_ref[lo:hi, :])
        acc = part if acc is None else acc + part
    return x + 0.5 * acc


def _const_spec(shape):
    nd = len(shape)
    return pl.BlockSpec(shape, lambda *_: (0,) * nd, pipeline_mode=pl.Buffered(1))


def _layer_spec(stacked, layer):
    return pl.BlockSpec((None,) + stacked.shape[1:], lambda *_: (layer, 0, 0), pipeline_mode=pl.Buffered(1))


def _row_spec(width):
    return pl.BlockSpec((ROW_TILE, width), lambda i: (i, 0))


def _rows_in_kernel(x_ref, g_ffa_ref, wg_ref, wu_ref, wd_ref, g_mix_ref, w_in_ref,
                    x_out_ref, qkv_ref, hy_ref):
    x = _swiglu_residual(x_ref[...], g_ffa_ref[...], wg_ref, wu_ref, wd_ref)
    x_out_ref[...] = x
    h = (_rms(x) * g_mix_ref[...]).astype(BF16)
    nq = 3 * NA_WIDTH
    qkv = _dot(h, w_in_ref[:, :nq])
    qkv_ref[:, :NA_WIDTH] = (qkv[:, :NA_WIDTH] * NA_QSCALE).astype(BF16)
    qkv_ref[:, NA_WIDTH:] = qkv[:, NA_WIDTH:].astype(BF16)
    hy_ref[...] = _dot(h, w_in_ref[:, nq:])


def _rows_in(x, layer, g_ffa, wg, wu, wd, g_mix, w_in):
    m = x.shape[0]
    return pl.pallas_call(
        _rows_in_kernel,
        grid=(m // ROW_TILE,),
        in_specs=[_row_spec(D_MODEL)] + [_layer_spec(c, layer) for c in (g_ffa, wg, wu, wd, g_mix, w_in)],
        out_specs=[_row_spec(D_MODEL), _row_spec(3 * NA_WIDTH), _row_spec(3 * HY_WIDTH)],
        out_shape=[jax.ShapeDtypeStruct((m, D_MODEL), F32),
                   jax.ShapeDtypeStruct((m, 3 * NA_WIDTH), BF16),
                   jax.ShapeDtypeStruct((m, 3 * HY_WIDTH), F32)],
        compiler_params=pltpu.CompilerParams(dimension_semantics=("parallel",), vmem_limit_bytes=VMEM_LIMIT),
        name="rows_in",
    )(x, g_ffa, wg, wu, wd, g_mix, w_in)


def _rows_out_kernel(x_ref, yna_ref, yhy_ref, p_ref, g_out_ref, w_out_ref, g_ffb_ref, wg_ref, wu_ref, wd_ref,
                     g_ple_ref, w_pg_ref, w_pp_ref, g_fin_ref, o_ref, *, final):
    g_out = g_out_ref[...]
    y = jnp.concatenate([(yna_ref[...] * g_out[:, :NA_WIDTH]).astype(BF16),
                         (_rms(yhy_ref[...]) * g_out[:, NA_WIDTH:]).astype(BF16)], axis=-1)
    x = x_ref[...] + _dot(y, w_out_ref[...])
    x = _swiglu_residual(x, g_ffb_ref[...], wg_ref, wu_ref, wd_ref)
    h = (_rms(x) * g_ple_ref[...]).astype(BF16)
    gate = jax.nn.sigmoid(_dot(h, w_pg_ref[...]))
    x = x + gate * _dot(p_ref[...].astype(BF16), w_pp_ref[...])
    if final:
        x = _rms(x) * g_fin_ref[...]
    o_ref[...] = x


def _rows_out(x, yna, yhy, p, layer, g_out, w_out, g_ffb, wg, wu, wd, g_ple, w_pg, w_pp, g_fin, final):
    m = x.shape[0]
    stacks = (g_out, w_out, g_ffb, wg, wu, wd, g_ple, w_pg, w_pp)
    return pl.pallas_call(
        functools.partial(_rows_out_kernel, final=final),
        grid=(m // ROW_TILE,),
        in_specs=[_row_spec(D_MODEL), _row_spec(NA_WIDTH), _row_spec(HY_WIDTH),
                  pl.BlockSpec((None, ROW_TILE, PLE_DIM), lambda i: (layer, i, 0))]
                 + [_layer_spec(c, layer) for c in stacks] + [_const_spec(g_fin.shape)],
        out_specs=_row_spec(D_MODEL),
        out_shape=jax.ShapeDtypeStruct((m, D_MODEL), F32),
        compiler_params=pltpu.CompilerParams(dimension_semantics=("parallel",), vmem_limit_bytes=VMEM_LIMIT),
        name="rows_out",
    )(x, yna, yhy, p, *stacks, g_fin)


def _na_key_base(g, rows):
    return jnp.clip(NA_QROWS * g - NA_KH // 2, 0, rows - NA_KROWS)


def _na_bias_tables(rpb, rows):
    n_groups = rows // NA_QROWS
    tables = []
    qc = np.arange(GRID_W)
    cs = np.clip(qc - NA_KW // 2, 0, GRID_W - NA_KW)
    kc = np.arange(GRID_W)
    col_ok = (kc[None, :] >= cs[:, None]) & (kc[None, :] < cs[:, None] + NA_KW)
    dc_idx = np.clip(kc[None, :] - qc[:, None] + NA_KW - 1, 0, 2 * NA_KW - 2)
    dc_hot = jnp.asarray(np.eye(2 * NA_KW - 1, dtype=np.float32)[dc_idx])
    blocks = jnp.einsum("hab,qcb->haqc", rpb, dc_hot, precision=lax.Precision.HIGHEST) * LOG2E
    blocks = jnp.where(jnp.asarray(col_ok)[None, None], blocks, -jnp.inf)
    hidden = jnp.full((NA_HEADS, GRID_W, GRID_W), -jnp.inf, F32)
    for g in (0, 1, n_groups - 1):
        r = NA_QROWS * g + np.arange(NA_QROWS)
        base = int(np.clip(NA_QROWS * g - NA_KH // 2, 0, rows - NA_KROWS))
        krow = base + np.arange(NA_KROWS)
        rs = np.clip(r - NA_KH // 2, 0, rows - NA_KH)
        row_ok = (krow[None, :] >= rs[:, None]) & (krow[None, :] < rs[:, None] + NA_KH)
        dr_idx = krow[None, :] - r[:, None] + NA_KH - 1
        strips = [jnp.concatenate([blocks[:, dr_idx[q, k]] if row_ok[q, k] else hidden for k in range(NA_KROWS)],
                                  axis=-1) for q in range(NA_QROWS)]
        tables.append(jnp.concatenate(strips, axis=1))
    return jnp.stack(tables)


def _na_kernel(q_ref, k_ref, v_ref, bias_ref, o_ref, *, rows):
    g = pl.program_id(1)
    start = pl.multiple_of(_na_key_base(g, rows) * GRID_W, GRID_W)
    nk = NA_KROWS * GRID_W
    lane = lax.broadcasted_iota(jnp.int32, (1, LANES), 1)
    outs = []
    for pair in range(NA_HEADS // 2):
        cols = slice(pair * LANES, (pair + 1) * LANES)
        qp = q_ref[0, :, cols]
        kp = k_ref[0, pl.ds(start, nk), cols]
        vp = v_ref[0, pl.ds(start, nk), cols]
        halves = []
        for sub in range(2):
            mine = (lane < NA_HEAD_DIM) if sub == 0 else (lane >= NA_HEAD_DIM)
            ones_lane = NA_HEAD_DIM * (1 - sub)
            s = lax.dot_general(jnp.where(mine, qp, 0), kp, (((1,), (1,)), ((), ())), preferred_element_type=F32)
            s = s + bias_ref[0, 2 * pair + sub]
            e = jnp.exp2(s - jnp.max(s, axis=-1, keepdims=True)).astype(BF16)
            o = _dot(e, jnp.where(mine, vp, jnp.where(lane == ones_lane, 1, 0).astype(BF16)))
            halves.append(o * (1.0 / o[:, ones_lane:ones_lane + 1]))
        outs.append(jnp.where(lane < NA_HEAD_DIM, halves[0], halves[1]))
    o_ref[0] = _rms(jnp.concatenate(outs, axis=-1))


def _na(qkv, bias, batch, seq):
    rows = seq // GRID_W
    n_groups = rows // NA_QROWS
    tq = NA_QROWS * GRID_W
    tk = NA_KROWS * GRID_W

    def pattern(b, g):
        return (jnp.where(g == 0, 0, jnp.where(g == n_groups - 1, 2, 1)), 0, 0, 0)

    return pl.pallas_call(
        functools.partial(_na_kernel, rows=rows),
        grid=(batch, n_groups),
        in_specs=[pl.BlockSpec((1, tq, NA_WIDTH), lambda b, g: (b, g, 0)),
                  pl.BlockSpec((1, seq, NA_WIDTH), lambda b, g: (b, 0, 1)),
                  pl.BlockSpec((1, seq, NA_WIDTH), lambda b, g: (b, 0, 2)),
                  pl.BlockSpec((1, NA_HEADS, tq, tk), pattern)],
        out_specs=pl.BlockSpec((1, tq, NA_WIDTH), lambda b, g: (b, g, 0)),
        out_shape=jax.ShapeDtypeStruct((batch, seq, NA_WIDTH), F32),
        compiler_params=pltpu.CompilerParams(dimension_semantics=("parallel", "arbitrary"),
                                             vmem_limit_bytes=VMEM_LIMIT),
        name="na",
    )(qkv, qkv, qkv, bias)


def _dft_mats(seq):
    half = seq // 2
    m = jnp.arange(half, dtype=jnp.int32)[None, :]
    a = jnp.arange(half // DFT_SPLIT, dtype=jnp.int32)[:, None]
    b = jnp.arange(DFT_SPLIT, dtype=jnp.int32)[:, None]
    ang_a = ((DFT_SPLIT * a * m) % seq).astype(F32) * (2.0 * math.pi / seq)
    ang_b = ((b * m) % seq).astype(F32) * (2.0 * math.pi / seq)
    ca, sa = jnp.cos(ang_a)[:, None, :], jnp.sin(ang_a)[:, None, :]
    cb, sb = jnp.cos(ang_b)[None, :, :], jnp.sin(ang_b)[None, :, :]
    cmat = (ca * cb - sa * sb).reshape(half, half)
    smat = (sa * cb + ca * sb).reshape(half, half)
    ang_t = jnp.arange(half, dtype=F32)[:, None] * (math.pi / seq)
    return cmat.astype(BF16), smat.astype(BF16), jnp.cos(ang_t), jnp.sin(ang_t)


def _filter_features(seq):
    bands = (FILTER_EMB - 1) // 2
    t = jnp.linspace(0.0, 1.0, seq, dtype=F32)[:, None]
    w = 2.0 * math.pi * jnp.arange(seq, dtype=F32)[:, None] / seq
    f = jnp.linspace(1e-4, bands - 1, bands, dtype=F32)[None, :]
    z = jnp.concatenate([t, jnp.cos(f * w), -jnp.sin(f * w)], axis=-1)
    return t, jnp.pad(z, ((0, 0), (0, LANES - FILTER_EMB)))


def _half_spectra(xe, xo, c_ref, s_ref, ct, st):
    nc = xe.shape[1]
    x = jnp.concatenate([xe, xo], axis=-1)
    rc = _dot(c_ref[...], x)
    rs = _dot(s_ref[...], x)
    ae, ao = rc[:, :nc], rc[:, nc:]
    be, bo = rs[:, :nc], rs[:, nc:]
    tr = ct * ao - st * bo
    ti = ct * bo + st * ao
    return ae + tr, -(be + ti), ae - tr, ti - be


def _alternating(n):
    row = lax.broadcasted_iota(jnp.int32, (n, 1), 0)
    return jnp.where((row & 1) == 0, 1.0, -1.0)


def _filt_kernel(z_ref, w1_ref, b1_ref, w2_ref, b2_ref, w3_ref, b3_ref, w4_ref, fr_ref, t_ref,
                 dl_ref, c_ref, s_ref, ct_ref, st_ref, kpr_ref, kpi_ref, kqr_ref, kqi_ref, kmid_ref):
    half = z_ref.shape[0] // 2
    n_fft = 4 * half
    hp = lax.Precision.HIGHEST
    fr = fr_ref[...]
    h = jnp.sin(fr * (jnp.dot(z_ref[...], w1_ref[...], precision=hp, preferred_element_type=F32) + b1_ref[...]))
    h = jnp.sin(fr * (jnp.dot(h, w2_ref[...], precision=hp, preferred_element_type=F32) + b2_ref[...]))
    h = jnp.sin(fr * (jnp.dot(h, w3_ref[...], precision=hp, preferred_element_type=F32) + b3_ref[...]))
    h = jnp.dot(h, w4_ref[...], precision=hp, preferred_element_type=F32)
    decay = jnp.exp(-t_ref[...] * dl_ref[...])
    h = h * jnp.concatenate([decay, decay], axis=-1)
    he, ho = h[:half], h[half:]
    first = lax.broadcasted_iota(jnp.int32, (half, 1), 0) == 0
    hf0, hb0 = he[:, :HY_WIDTH], he[:, HY_WIDTH:]
    tot = jnp.sum(jnp.abs(he) + jnp.abs(ho), axis=0, keepdims=True)
    lag0 = jnp.sum(jnp.where(first, jnp.abs(hf0 + hb0) - jnp.abs(hf0) - jnp.abs(hb0), 0.0), axis=0, keepdims=True)
    inv = 1.0 / (tot[:, :HY_WIDTH] + tot[:, HY_WIDTH:] + lag0)

    ct, st = ct_ref[...], st_ref[...]
    fpr, fpi, fqr, fqi = _half_spectra(he[:, :HY_WIDTH].astype(BF16), ho[:, :HY_WIDTH].astype(BF16),
                                       c_ref, s_ref, ct, st)
    bpr, bpi, bqr, bqi = _half_spectra(he[:, HY_WIDTH:].astype(BF16), ho[:, HY_WIDTH:].astype(BF16),
                                       c_ref, s_ref, ct, st)
    scale = inv * jnp.where(first, 1.0 / n_fft, 2.0 / n_fft)
    kpr_ref[...] = (fpr + bpr) * scale
    kpi_ref[...] = (fpi - bpi) * scale
    kqr_ref[...] = (fqr + bqr) * scale
    kqi_ref[...] = (fqi - bqi) * scale
    alt = _alternating(half)
    a1 = jnp.sum(he * alt, axis=0, keepdims=True)
    b1 = jnp.sum(ho * alt, axis=0, keepdims=True)
    mid = inv * (2.0 / n_fft)
    kmid_ref[0:1, :] = (a1[:, :HY_WIDTH] + a1[:, HY_WIDTH:]) * mid
    kmid_ref[1:2, :] = (b1[:, HY_WIDTH:] - b1[:, :HY_WIDTH]) * mid


def _filter_spectrum(z, t, deltas, cmat, smat, ct, st, w1, b1, w2, b2, w3, b3, w4, freq):
    half = z.shape[0] // 2
    pad2 = lambda w, r, c: jnp.pad(w, ((0, r - w.shape[0]), (0, c - w.shape[1])))
    even_then_odd = lambda a: jnp.concatenate([a[0::2], a[1::2]], axis=0)
    args = (even_then_odd(z), pad2(w1, LANES, LANES), pad2(b1[None], 1, LANES), pad2(w2, LANES, LANES),
            pad2(b2[None], 1, LANES), pad2(w3, LANES, LANES), pad2(b3[None], 1, LANES),
            pad2(w4, LANES, 2 * HY_WIDTH), pad2(freq[None], 1, LANES), even_then_odd(t), deltas, cmat, smat, ct, st)
    spec = jax.ShapeDtypeStruct((half, HY_WIDTH), F32)
    return pl.pallas_call(
        _filt_kernel,
        out_shape=[spec, spec, spec, spec, jax.ShapeDtypeStruct((2, HY_WIDTH), F32)],
        compiler_params=pltpu.CompilerParams(vmem_limit_bytes=VMEM_LIMIT),
        name="filt",
    )(*args)


def _short_conv(u_ref, w_ref, b_ref):
    half = u_ref.shape[0] // 2
    row = lax.broadcasted_iota(jnp.int32, (half, 1), 0)
    ue = u_ref[pl.ds(0, half, stride=2), :]
    uo = u_ref[pl.ds(1, half, stride=2), :]
    uo_prev = jnp.where(row == 0, 0.0, pltpu.roll(uo, 1, 0))
    ue_next = jnp.where(row == half - 1, 0.0, pltpu.roll(ue, half - 1, 0))
    w0, w1, w2, b = w_ref[0:1, :], w_ref[1:2, :], w_ref[2:3, :], b_ref[...]
    return uo_prev * w0 + ue * w1 + uo * w2 + b, ue * w0 + uo * w1 + ue_next * w2 + b


def _hyena_kernel(x0_ref, x1_ref, hv_ref, w0_ref, w1_ref, wv_ref, b0_ref, b1_ref, bv_ref,
                  c_ref, s_ref, ct_ref, st_ref, kpr_ref, kpi_ref, kqr_ref, kqi_ref, kmid_ref, hb_ref, o_ref):
    half = x0_ref.shape[0] // 2
    nc = x0_ref.shape[1]
    ct, st = ct_ref[...], st_ref[...]
    hve, hvo = _short_conv(hv_ref, wv_ref, bv_ref)
    x1e, x1o = _short_conv(x1_ref, w1_ref, b1_ref)
    ve, vo = hve * x1e, hvo * x1o
    pr, pi, qr, qi = _half_spectra(ve.astype(BF16), vo.astype(BF16), c_ref, s_ref, ct, st)
    kpr, kpi, kqr, kqi = kpr_ref[...], kpi_ref[...], kqr_ref[...], kqi_ref[...]
    ypr, ypi = pr * kpr - pi * kpi, pr * kpi + pi * kpr
    yqr, yqi = qr * kqr - qi * kqi, qr * kqi + qi * kqr
    dr, di = ypr - yqr, ypi - yqi
    zr = jnp.concatenate([(ypr + yqr).astype(BF16), (dr * ct - di * st).astype(BF16)], axis=-1)
    zi = jnp.concatenate([(ypi + yqi).astype(BF16), (dr * st + di * ct).astype(BF16)], axis=-1)
    y = _dot(c_ref[...], zr) - _dot(s_ref[...], zi)
    alt = _alternating(half)
    a1 = jnp.sum(ve * alt, axis=0, keepdims=True)
    b1 = jnp.sum(vo * alt, axis=0, keepdims=True)
    kmr, kmi = kmid_ref[0:1, :], kmid_ref[1:2, :]
    ye = y[:, :nc] + alt * (a1 * kmr + b1 * kmi)
    yo = y[:, nc:] - alt * (a1 * kmi - b1 * kmr)
    x0e, x0o = _short_conv(x0_ref, w0_ref, b0_ref)
    hb = hb_ref[...]
    o_ref[pl.ds(0, half, stride=2), :] = x0e * (ye + ve * hb)
    o_ref[pl.ds(1, half, stride=2), :] = x0o * (yo + vo * hb)


def _hyena(hy, w_sc, b_sc, cmat, smat, ct, st, kspec, hy_bias, batch, seq):
    kpr, kpi, kqr, kqi, kmid = kspec
    nt = HY_WIDTH // HY_CTILE
    half = seq // 2
    blk = lambda off: pl.BlockSpec((None, seq, HY_CTILE), lambda j, b: (b, 0, off * nt + j))
    wblk = lambda off: pl.BlockSpec((3, HY_CTILE), lambda j, b: (0, off * nt + j))
    bblk = lambda off: pl.BlockSpec((1, HY_CTILE), lambda j, b: (0, off * nt + j))
    chan = lambda r: pl.BlockSpec((r, HY_CTILE), lambda j, b: (0, j))
    spec = pl.BlockSpec((half, HY_CTILE), lambda j, b: (0, j), pipeline_mode=pl.Buffered(1))
    return pl.pallas_call(
        _hyena_kernel,
        grid=(nt, batch),
        in_specs=[blk(0), blk(1), blk(2), wblk(0), wblk(1), wblk(2), bblk(0), bblk(1), bblk(2),
                  _const_spec(cmat.shape), _const_spec(smat.shape), _const_spec(ct.shape), _const_spec(st.shape),
                  spec, spec, spec, spec, chan(2), chan(1)],
        out_specs=pl.BlockSpec((None, seq, HY_CTILE), lambda j, b: (b, 0, j)),
        out_shape=jax.ShapeDtypeStruct((batch, seq, HY_WIDTH), F32),
        compiler_params=pltpu.CompilerParams(dimension_semantics=("arbitrary", "arbitrary"),
                                             vmem_limit_bytes=VMEM_LIMIT),
        name="hyena",
    )(hy, hy, hy, w_sc, w_sc, w_sc, b_sc, b_sc, b_sc, cmat, smat, ct, st, kpr, kpi, kqr, kqi, kmid, hy_bias)


def kernel(x, p, g_ffa, w_ffa_gate, w_ffa_up, w_ffa_down, g_mix, w_in, na_rpb, w_sc, b_sc, w_f1, b_f1, w_f2, b_f2, w_f3, b_f3, w_f4, filt_freq, hy_bias, g_out, w_out, g_ffb, w_ffb_gate, w_ffb_up, w_ffb_down, g_ple, w_ple_gate, w_ple_proj, g_final):
    batch, seq, _ = x.shape
    depth = p.shape[0]
    m = batch * seq
    rows = seq // GRID_W
    bf = lambda w: w.astype(BF16)
    vec = lambda g: g[:, None, :]

    cmat, smat, ct, st = _dft_mats(seq)
    t, z = _filter_features(seq)
    deltas = jnp.abs(jnp.linspace(MIN_DECAY, MAX_DECAY, HY_WIDTH, dtype=F32))[None, :]

    in_params = (vec(g_ffa), bf(w_ffa_gate), bf(w_ffa_up), bf(w_ffa_down), vec(g_mix), bf(w_in))
    out_params = (vec(g_out), bf(w_out), vec(g_ffb), bf(w_ffb_gate), bf(w_ffb_up), bf(w_ffb_down),
                  vec(g_ple), bf(w_ple_gate), bf(w_ple_proj))
    pr = p.reshape(depth, m, PLE_DIM)

    xr = x.reshape(m, D_MODEL)
    for i in range(depth):
        xr, qkv, hy = _rows_in(xr, i, *in_params)
        bias = _na_bias_tables(na_rpb[i], rows)
        yna = _na(qkv.reshape(batch, seq, 3 * NA_WIDTH), bias, batch, seq)
        kspec = _filter_spectrum(z, t, deltas, cmat, smat, ct, st, w_f1[i], b_f1[i], w_f2[i], b_f2[i],
                                 w_f3[i], b_f3[i], w_f4[i], filt_freq[i])
        yhy = _hyena(hy.reshape(batch, seq, 3 * HY_WIDTH), w_sc[i], b_sc[i][None], cmat, smat, ct, st, kspec,
                     hy_bias[i][None], batch, seq)
        xr = _rows_out(xr, yna.reshape(m, NA_WIDTH), yhy.reshape(m, HY_WIDTH), pr, i, *out_params,
                       g_final[None], final=(i == depth - 1))
    return xr.reshape(batch, seq, D_MODEL)
```

```python
import functools
import math

import numpy as np
import jax
import jax.numpy as jnp
from jax import lax
from jax.experimental import pallas as pl
from jax.experimental.pallas import tpu as pltpu

F32 = jnp.float32
BF16 = jnp.bfloat16

D_MODEL = 1024
GRID_W = 64
NA_HEADS = 8
NA_HEAD_DIM = 64
NA_WIDTH = NA_HEADS * NA_HEAD_DIM
NA_KH = 8
NA_KW = 16
HY_WIDTH = D_MODEL - NA_WIDTH
FILTER_EMB = 33
DECAY_TARGET = 1e-2
FAST_DECAY_PCT = 0.3
SLOW_DECAY_PCT = 1.5
MAX_DECAY = math.log(DECAY_TARGET) / FAST_DECAY_PCT
MIN_DECAY = math.log(DECAY_TARGET) / SLOW_DECAY_PCT
D_FF = 2816
PLE_DIM = 256
EPS = 1e-6

LANES = 128
ROW_TILE = 512
LOG2E = math.log2(math.e)
NA_QSCALE = NA_HEAD_DIM ** -0.5 * LOG2E
FF_CHUNKS = ((0, 1024), (1024, 2048), (2048, D_FF))
NA_QROWS = 4
NA_KROWS = 12
HY_CTILE = 128
DFT_SPLIT = 64
VMEM_LIMIT = 56 * 1024 * 1024


def _rms(x):
    return x * lax.rsqrt(jnp.mean(x * x, axis=-1, keepdims=True) + EPS)


def _dot(a, b):
    return jnp.dot(a, b, preferred_element_type=F32)


def _swiglu_residual(x, g, wg_ref, wu_ref, wd_ref):
    h = (_rms(x) * g).astype(BF16)
    acc = None
    for lo, hi in FF_CHUNKS:
        gate = _dot(h, wg_ref[:, lo:hi])
        up = _dot(h, wu_ref[:, lo:hi])
        act = (gate * jax.nn.sigmoid(gate) * up).astype(BF16)
        part = _dot(act, wd_ref[lo:hi, :])
        acc = part if acc is None else acc + part
    return x + 0.5 * acc


def _const_spec(shape):
    nd = len(shape)
    return pl.BlockSpec(shape, lambda *_: (0,) * nd, pipeline_mode=pl.Buffered(1))


def _layer_spec(stacked, layer):
    return pl.BlockSpec((None,) + stacked.shape[1:], lambda *_: (layer, 0, 0), pipeline_mode=pl.Buffered(1))


def _row_spec(width):
    return pl.BlockSpec((ROW_TILE, width), lambda i: (i, 0))


def _rows_in_kernel(x_ref, g_ffa_ref, wg_ref, wu_ref, wd_ref, g_mix_ref, w_in_ref,
                    x_out_ref, qkv_ref, hy_ref):
    x = _swiglu_residual(x_ref[...], g_ffa_ref[...], wg_ref, wu_ref, wd_ref)
    x_out_ref[...] = x
    h = (_rms(x) * g_mix_ref[...]).astype(BF16)
    nq = 3 * NA_WIDTH
    qkv = _dot(h, w_in_ref[:, :nq])
    qkv_ref[:, :NA_WIDTH] = (qkv[:, :NA_WIDTH] * NA_QSCALE).astype(BF16)
    qkv_ref[:, NA_WIDTH:] = qkv[:, NA_WIDTH:].astype(BF16)
    hy_ref[...] = _dot(h, w_in_ref[:, nq:])


def _rows_in(x, layer, g_ffa, wg, wu, wd, g_mix, w_in):
    m = x.shape[0]
    return pl.pallas_call(
        _rows_in_kernel,
        grid=(m // ROW_TILE,),
        in_specs=[_row_spec(D_MODEL)] + [_layer_spec(c, layer) for c in (g_ffa, wg, wu, wd, g_mix, w_in)],
        out_specs=[_row_spec(D_MODEL), _row_spec(3 * NA_WIDTH), _row_spec(3 * HY_WIDTH)],
        out_shape=[jax.ShapeDtypeStruct((m, D_MODEL), F32),
                   jax.ShapeDtypeStruct((m, 3 * NA_WIDTH), BF16),
                   jax.ShapeDtypeStruct((m, 3 * HY_WIDTH), F32)],
        compiler_params=pltpu.CompilerParams(dimension_semantics=("parallel",), vmem_limit_bytes=VMEM_LIMIT),
        name="rows_in",
    )(x, g_ffa, wg, wu, wd, g_mix, w_in)


def _rows_out_kernel(x_ref, yna_ref, yhy_ref, p_ref, g_out_ref, w_out_ref, g_ffb_ref, wg_ref, wu_ref, wd_ref,
                     g_ple_ref, w_pg_ref, w_pp_ref, g_fin_ref, o_ref, *, final):
    g_out = g_out_ref[...]
    y = jnp.concatenate([(yna_ref[...] * g_out[:, :NA_WIDTH]).astype(BF16),
                         (_rms(yhy_ref[...]) * g_out[:, NA_WIDTH:]).astype(BF16)], axis=-1)
    x = x_ref[...] + _dot(y, w_out_ref[...])
    x = _swiglu_residual(x, g_ffb_ref[...], wg_ref, wu_ref, wd_ref)
    h = (_rms(x) * g_ple_ref[...]).astype(BF16)
    gate = jax.nn.sigmoid(_dot(h, w_pg_ref[...]))
    x = x + gate * _dot(p_ref[...].astype(BF16), w_pp_ref[...])
    if final:
        x = _rms(x) * g_fin_ref[...]
    o_ref[...] = x


def _rows_out(x, yna, yhy, p, layer, g_out, w_out, g_ffb, wg, wu, wd, g_ple, w_pg, w_pp, g_fin, final):
    m = x.shape[0]
    stacks = (g_out, w_out, g_ffb, wg, wu, wd, g_ple, w_pg, w_pp)
    return pl.pallas_call(
        functools.partial(_rows_out_kernel, final=final),
        grid=(m // ROW_TILE,),
        in_specs=[_row_spec(D_MODEL), _row_spec(NA_WIDTH), _row_spec(HY_WIDTH),
                  pl.BlockSpec((None, ROW_TILE, PLE_DIM), lambda i: (layer, i, 0))]
                 + [_layer_spec(c, layer) for c in stacks] + [_const_spec(g_fin.shape)],
        out_specs=_row_spec(D_MODEL),
        out_shape=jax.ShapeDtypeStruct((m, D_MODEL), F32),
        compiler_params=pltpu.CompilerParams(dimension_semantics=("parallel",), vmem_limit_bytes=VMEM_LIMIT),
        name="rows_out",
    )(x, yna, yhy, p, *stacks, g_fin)


def _na_key_base(g, rows):
    return jnp.clip(NA_QROWS * g - NA_KH // 2, 0, rows - NA_KROWS)


def _na_bias_tables(rpb, rows):
    n_groups = rows // NA_QROWS
    tables = []
    qc = np.arange(GRID_W)
    cs = np.clip(qc - NA_KW // 2, 0, GRID_W - NA_KW)
    kc = np.arange(GRID_W)
    col_ok = (kc[None, :] >= cs[:, None]) & (kc[None, :] < cs[:, None] + NA_KW)
    dc_idx = np.clip(kc[None, :] - qc[:, None] + NA_KW - 1, 0, 2 * NA_KW - 2)
    dc_hot = jnp.asarray(np.eye(2 * NA_KW - 1, dtype=np.float32)[dc_idx])
    blocks = jnp.einsum("hab,qcb->haqc", rpb, dc_hot, precision=lax.Precision.HIGHEST) * LOG2E
    blocks = jnp.where(jnp.asarray(col_ok)[None, None], blocks, -jnp.inf)
    hidden = jnp.full((NA_HEADS, GRID_W, GRID_W), -jnp.inf, F32)
    for g in (0, 1, n_groups - 1):
        r = NA_QROWS * g + np.arange(NA_QROWS)
        base = int(np.clip(NA_QROWS * g - NA_KH // 2, 0, rows - NA_KROWS))
        krow = base + np.arange(NA_KROWS)
        rs = np.clip(r - NA_KH // 2, 0, rows - NA_KH)
        row_ok = (krow[None, :] >= rs[:, None]) & (krow[None, :] < rs[:, None] + NA_KH)
        dr_idx = krow[None, :] - r[:, None] + NA_KH - 1
        strips = [jnp.concatenate([blocks[:, dr_idx[q, k]] if row_ok[q, k] else hidden for k in range(NA_KROWS)],
                                  axis=-1) for q in range(NA_QROWS)]
        tables.append(jnp.concatenate(strips, axis=1))
    return jnp.stack(tables)


def _na_kernel(q_ref, k_ref, v_ref, bias_ref, o_ref, *, rows):
    g = pl.program_id(1)
    start = pl.multiple_of(_na_key_base(g, rows) * GRID_W, GRID_W)
    nk = NA_KROWS * GRID_W
    tq = NA_QROWS * GRID_W
    lo_half = lax.broadcasted_iota(jnp.int32, (1, LANES), 1) < NA_HEAD_DIM
    outs = []
    for pair in range(NA_HEADS // 2):
        cols = slice(pair * LANES, (pair + 1) * LANES)
        qp = q_ref[0, :, cols]
        kp = k_ref[0, pl.ds(start, nk), cols]
        vp = v_ref[0, pl.ds(start, nk), cols]
        q2 = jnp.concatenate([jnp.where(lo_half, qp, 0), jnp.where(lo_half, 0, qp)], axis=0)
        s = lax.dot_general(q2, kp, (((1,), (1,)), ((), ())), preferred_element_type=F32)
        s = s + bias_ref[0, 2 * pair:2 * pair + 2].reshape(2 * tq, nk)
        e = jnp.exp2(s - jnp.max(s, axis=-1, keepdims=True))
        inv = 1.0 / jnp.sum(e, axis=-1, keepdims=True)
        e = e.astype(BF16)
        e2 = jnp.concatenate([e[:tq], e[tq:]], axis=1)
        v2 = jnp.concatenate([jnp.where(lo_half, vp, 0), jnp.where(lo_half, 0, vp)], axis=0)
        outs.append(_dot(e2, v2) * jnp.where(lo_half, inv[:tq], inv[tq:]))
    o_ref[0] = _rms(jnp.concatenate(outs, axis=-1))


def _na(qkv, bias, batch, seq):
    rows = seq // GRID_W
    n_groups = rows // NA_QROWS
    tq = NA_QROWS * GRID_W
    tk = NA_KROWS * GRID_W

    def pattern(b, g):
        return (jnp.where(g == 0, 0, jnp.where(g == n_groups - 1, 2, 1)), 0, 0, 0)

    return pl.pallas_call(
        functools.partial(_na_kernel, rows=rows),
        grid=(batch, n_groups),
        in_specs=[pl.BlockSpec((1, tq, NA_WIDTH), lambda b, g: (b, g, 0)),
                  pl.BlockSpec((1, seq, NA_WIDTH), lambda b, g: (b, 0, 1)),
                  pl.BlockSpec((1, seq, NA_WIDTH), lambda b, g: (b, 0, 2)),
                  pl.BlockSpec((1, NA_HEADS, tq, tk), pattern)],
        out_specs=pl.BlockSpec((1, tq, NA_WIDTH), lambda b, g: (b, g, 0)),
        out_shape=jax.ShapeDtypeStruct((batch, seq, NA_WIDTH), F32),
        compiler_params=pltpu.CompilerParams(dimension_semantics=("parallel", "arbitrary"),
                                             vmem_limit_bytes=VMEM_LIMIT),
        name="na",
    )(qkv, qkv, qkv, bias)


def _dft_mats(seq):
    half = seq // 2
    m = jnp.arange(half, dtype=jnp.int32)[None, :]
    a = jnp.arange(half // DFT_SPLIT, dtype=jnp.int32)[:, None]
    b = jnp.arange(DFT_SPLIT, dtype=jnp.int32)[:, None]
    ang_a = ((DFT_SPLIT * a * m) % seq).astype(F32) * (2.0 * math.pi / seq)
    ang_b = ((b * m) % seq).astype(F32) * (2.0 * math.pi / seq)
    ca, sa = jnp.cos(ang_a)[:, None, :], jnp.sin(ang_a)[:, None, :]
    cb, sb = jnp.cos(ang_b)[None, :, :], jnp.sin(ang_b)[None, :, :]
    cmat = (ca * cb - sa * sb).reshape(half, half)
    smat = (sa * cb + ca * sb).reshape(half, half)
    ang_t = jnp.arange(half, dtype=F32)[:, None] * (math.pi / seq)
    return cmat.astype(BF16), smat.astype(BF16), jnp.cos(ang_t), jnp.sin(ang_t)


def _filter_features(seq):
    bands = (FILTER_EMB - 1) // 2
    t = jnp.linspace(0.0, 1.0, seq, dtype=F32)[:, None]
    w = 2.0 * math.pi * jnp.arange(seq, dtype=F32)[:, None] / seq
    f = jnp.linspace(1e-4, bands - 1, bands, dtype=F32)[None, :]
    z = jnp.concatenate([t, jnp.cos(f * w), -jnp.sin(f * w)], axis=-1)
    return t, jnp.pad(z, ((0, 0), (0, LANES - FILTER_EMB)))


def _half_spectra(xe, xo, c_ref, s_ref, ct, st):
    nc = xe.shape[1]
    x = jnp.concatenate([xe, xo], axis=-1)
    rc = _dot(c_ref[...], x)
    rs = _dot(s_ref[...], x)
    ae, ao = rc[:, :nc], rc[:, nc:]
    be, bo = rs[:, :nc], rs[:, nc:]
    tr = ct * ao - st * bo
    ti = ct * bo + st * ao
    return ae + tr, -(be + ti), ae - tr, ti - be


def _alternating(n):
    row = lax.broadcasted_iota(jnp.int32, (n, 1), 0)
    return jnp.where((row & 1) == 0, 1.0, -1.0)


def _filt_kernel(z_ref, w1_ref, b1_ref, w2_ref, b2_ref, w3_ref, b3_ref, w4_ref, fr_ref, t_ref,
                 dl_ref, c_ref, s_ref, ct_ref, st_ref, kpr_ref, kpi_ref, kqr_ref, kqi_ref, kmid_ref):
    half = z_ref.shape[0] // 2
    n_fft = 4 * half
    hp = lax.Precision.HIGHEST
    fr = fr_ref[...]
    h = jnp.sin(fr * (jnp.dot(z_ref[...], w1_ref[...], precision=hp, preferred_element_type=F32) + b1_ref[...]))
    h = jnp.sin(fr * (jnp.dot(h, w2_ref[...], precision=hp, preferred_element_type=F32) + b2_ref[...]))
    h = jnp.sin(fr * (jnp.dot(h, w3_ref[...], precision=hp, preferred_element_type=F32) + b3_ref[...]))
    h = jnp.dot(h, w4_ref[...], precision=hp, preferred_element_type=F32)
    decay = jnp.exp(-t_ref[...] * dl_ref[...])
    h = h * jnp.concatenate([decay, decay], axis=-1)
    he, ho = h[:half], h[half:]
    first = lax.broadcasted_iota(jnp.int32, (half, 1), 0) == 0
    hf0, hb0 = he[:, :HY_WIDTH], he[:, HY_WIDTH:]
    tot = jnp.sum(jnp.abs(he) + jnp.abs(ho), axis=0, keepdims=True)
    lag0 = jnp.sum(jnp.where(first, jnp.abs(hf0 + hb0) - jnp.abs(hf0) - jnp.abs(hb0), 0.0), axis=0, keepdims=True)
    inv = 1.0 / (tot[:, :HY_WIDTH] + tot[:, HY_WIDTH:] + lag0)

    ct, st = ct_ref[...], st_ref[...]
    fpr, fpi, fqr, fqi = _half_spectra(he[:, :HY_WIDTH].astype(BF16), ho[:, :HY_WIDTH].astype(BF16),
                                       c_ref, s_ref, ct, st)
    bpr, bpi, bqr, bqi = _half_spectra(he[:, HY_WIDTH:].astype(BF16), ho[:, HY_WIDTH:].astype(BF16),
                                       c_ref, s_ref, ct, st)
    scale = inv * jnp.where(first, 1.0 / n_fft, 2.0 / n_fft)
    kpr_ref[...] = (fpr + bpr) * scale
    kpi_ref[...] = (fpi - bpi) * scale
    kqr_ref[...] = (fqr + bqr) * scale
    kqi_ref[...] = (fqi - bqi) * scale
    alt = _alternating(half)
    a1 = jnp.sum(he * alt, axis=0, keepdims=True)
    b1 = jnp.sum(ho * alt, axis=0, keepdims=True)
    mid = inv * (2.0 / n_fft)
    kmid_ref[0:1, :] = (a1[:, :HY_WIDTH] + a1[:, HY_WIDTH:]) * mid
    kmid_ref[1:2, :] = (b1[:, HY_WIDTH:] - b1[:, :HY_WIDTH]) * mid


def _filter_spectrum(z, t, deltas, cmat, smat, ct, st, w1, b1, w2, b2, w3, b3, w4, freq):
    half = z.shape[0] // 2
    pad2 = lambda w, r, c: jnp.pad(w, ((0, r - w.shape[0]), (0, c - w.shape[1])))
    even_then_odd = lambda a: jnp.concatenate([a[0::2], a[1::2]], axis=0)
    args = (even_then_odd(z), pad2(w1, LANES, LANES), pad2(b1[None], 1, LANES), pad2(w2, LANES, LANES),
            pad2(b2[None], 1, LANES), pad2(w3, LANES, LANES), pad2(b3[None], 1, LANES),
            pad2(w4, LANES, 2 * HY_WIDTH), pad2(freq[None], 1, LANES), even_then_odd(t), deltas, cmat, smat, ct, st)
    spec = jax.ShapeDtypeStruct((half, HY_WIDTH), F32)
    return pl.pallas_call(
        _filt_kernel,
        out_shape=[spec, spec, spec, spec, jax.ShapeDtypeStruct((2, HY_WIDTH), F32)],
        compiler_params=pltpu.CompilerParams(vmem_limit_bytes=VMEM_LIMIT),
        name="filt",
    )(*args)


def _short_conv(u_ref, w_ref, b_ref):
    half = u_ref.shape[0] // 2
    row = lax.broadcasted_iota(jnp.int32, (half, 1), 0)
    ue = u_ref[pl.ds(0, half, stride=2), :]
    uo = u_ref[pl.ds(1, half, stride=2), :]
    uo_prev = jnp.where(row == 0, 0.0, pltpu.roll(uo, 1, 0))
    ue_next = jnp.where(row == half - 1, 0.0, pltpu.roll(ue, half - 1, 0))
    w0, w1, w2, b = w_ref[0:1, :], w_ref[1:2, :], w_ref[2:3, :], b_ref[...]
    return uo_prev * w0 + ue * w1 + uo * w2 + b, ue * w0 + uo * w1 + ue_next * w2 + b


def _hyena_kernel(x0_ref, x1_ref, hv_ref, w0_ref, w1_ref, wv_ref, b0_ref, b1_ref, bv_ref,
                  c_ref, s_ref, ct_ref, st_ref, kpr_ref, kpi_ref, kqr_ref, kqi_ref, kmid_ref, hb_ref, o_ref):
    half = x0_ref.shape[0] // 2
    nc = x0_ref.shape[1]
    ct, st = ct_ref[...], st_ref[...]
    hve, hvo = _short_conv(hv_ref, wv_ref, bv_ref)
    x1e, x1o = _short_conv(x1_ref, w1_ref, b1_ref)
    ve, vo = hve * x1e, hvo * x1o
    pr, pi, qr, qi = _half_spectra(ve.astype(BF16), vo.astype(BF16), c_ref, s_ref, ct, st)
    kpr, kpi, kqr, kqi = kpr_ref[...], kpi_ref[...], kqr_ref[...], kqi_ref[...]
    ypr, ypi = pr * kpr - pi * kpi, pr * kpi + pi * kpr
    yqr, yqi = qr * kqr - qi * kqi, qr * kqi + qi * kqr
    dr, di = ypr - yqr, ypi - yqi
    zr = jnp.concatenate([(ypr + yqr).astype(BF16), (dr * ct - di * st).astype(BF16)], axis=-1)
    zi = jnp.concatenate([(ypi + yqi).astype(BF16), (dr * st + di * ct).astype(BF16)], axis=-1)
    y = _dot(c_ref[...], zr) - _dot(s_ref[...], zi)
    alt = _alternating(half)
    a1 = jnp.sum(ve * alt, axis=0, keepdims=True)
    b1 = jnp.sum(vo * alt, axis=0, keepdims=True)
    kmr, kmi = kmid_ref[0:1, :], kmid_ref[1:2, :]
    ye = y[:, :nc] + alt * (a1 * kmr + b1 * kmi)
    yo = y[:, nc:] - alt * (a1 * kmi - b1 * kmr)
    x0e, x0o = _short_conv(x0_ref, w0_ref, b0_ref)
    hb = hb_ref[...]
    o_ref[pl.ds(0, half, stride=2), :] = x0e * (ye + ve * hb)
    o_ref[pl.ds(1, half, stride=2), :] = x0o * (yo + vo * hb)


def _hyena(hy, w_sc, b_sc, cmat, smat, ct, st, kspec, hy_bias, batch, seq):
    kpr, kpi, kqr, kqi, kmid = kspec
    nt = HY_WIDTH // HY_CTILE
    half = seq // 2
    blk = lambda off: pl.BlockSpec((None, seq, HY_CTILE), lambda j, b: (b, 0, off * nt + j))
    wblk = lambda off: pl.BlockSpec((3, HY_CTILE), lambda j, b: (0, off * nt + j))
    bblk = lambda off: pl.BlockSpec((1, HY_CTILE), lambda j, b: (0, off * nt + j))
    chan = lambda r: pl.BlockSpec((r, HY_CTILE), lambda j, b: (0, j))
    spec = pl.BlockSpec((half, HY_CTILE), lambda j, b: (0, j), pipeline_mode=pl.Buffered(1))
    return pl.pallas_call(
        _hyena_kernel,
        grid=(nt, batch),
        in_specs=[blk(0), blk(1), blk(2), wblk(0), wblk(1), wblk(2), bblk(0), bblk(1), bblk(2),
                  _const_spec(cmat.shape), _const_spec(smat.shape), _const_spec(ct.shape), _const_spec(st.shape),
                  spec, spec, spec, spec, chan(2), chan(1)],
        out_specs=pl.BlockSpec((None, seq, HY_CTILE), lambda j, b: (b, 0, j)),
        out_shape=jax.ShapeDtypeStruct((batch, seq, HY_WIDTH), F32),
        compiler_params=pltpu.CompilerParams(dimension_semantics=("arbitrary", "arbitrary"),
                                             vmem_limit_bytes=VMEM_LIMIT),
        name="hyena",
    )(hy, hy, hy, w_sc, w_sc, w_sc, b_sc, b_sc, b_sc, cmat, smat, ct, st, kpr, kpi, kqr, kqi, kmid, hy_bias)


def kernel(x, p, g_ffa, w_ffa_gate, w_ffa_up, w_ffa_down, g_mix, w_in, na_rpb, w_sc, b_sc, w_f1, b_f1, w_f2, b_f2, w_f3, b_f3, w_f4, filt_freq, hy_bias, g_out, w_out, g_ffb, w_ffb_gate, w_ffb_up, w_ffb_down, g_ple, w_ple_gate, w_ple_proj, g_final):
    batch, seq, _ = x.shape
    depth = p.shape[0]
    m = batch * seq
    rows = seq // GRID_W
    bf = lambda w: w.astype(BF16)
    vec = lambda g: g[:, None, :]

    cmat, smat, ct, st = _dft_mats(seq)
    t, z = _filter_features(seq)
    deltas = jnp.abs(jnp.linspace(MIN_DECAY, MAX_DECAY, HY_WIDTH, dtype=F32))[None, :]

    in_params = (vec(g_ffa), bf(w_ffa_gate), bf(w_ffa_up), bf(w_ffa_down), vec(g_mix), bf(w_in))
    out_params = (vec(g_out), bf(w_out), vec(g_ffb), bf(w_ffb_gate), bf(w_ffb_up), bf(w_ffb_down),
                  vec(g_ple), bf(w_ple_gate), bf(w_ple_proj))
    pr = p.reshape(depth, m, PLE_DIM)

    xr = x.reshape(m, D_MODEL)
    for i in range(depth):
        xr, qkv, hy = _rows_in(xr, i, *in_params)
        bias = _na_bias_tables(na_rpb[i], rows)
        yna = _na(qkv.reshape(batch, seq, 3 * NA_WIDTH), bias, batch, seq)
        kspec = _filter_spectrum(z, t, deltas, cmat, smat, ct, st, w_f1[i], b_f1[i], w_f2[i], b_f2[i],
                                 w_f3[i], b_f3[i], w_f4[i], filt_freq[i])
        yhy = _hyena(hy.reshape(batch, seq, 3 * HY_WIDTH), w_sc[i], b_sc[i][None], cmat, smat, ct, st, kspec,
                     hy_bias[i][None], batch, seq)
        xr = _rows_out(xr, yna.reshape(m, NA_WIDTH), yhy.reshape(m, HY_WIDTH), pr, i, *out_params,
                       g_final[None], final=(i == depth - 1))
    return xr.reshape(batch, seq, D_MODEL)
```

```python
import functools
import math

import numpy as np
import jax
import jax.numpy as jnp
from jax import lax
from jax.experimental import pallas as pl
from jax.experimental.pallas import tpu as pltpu

F32 = jnp.float32
BF16 = jnp.bfloat16

D_MODEL = 1024
GRID_W = 64
NA_HEADS = 8
NA_HEAD_DIM = 64
NA_WIDTH = NA_HEADS * NA_HEAD_DIM
NA_KH = 8
NA_KW = 16
HY_WIDTH = D_MODEL - NA_WIDTH
FILTER_EMB = 33
DECAY_TARGET = 1e-2
FAST_DECAY_PCT = 0.3
SLOW_DECAY_PCT = 1.5
MAX_DECAY = math.log(DECAY_TARGET) / FAST_DECAY_PCT
MIN_DECAY = math.log(DECAY_TARGET) / SLOW_DECAY_PCT
D_FF = 2816
PLE_DIM = 256
EPS = 1e-6

LANES = 128
ROW_TILE = 512
LOG2E = math.log2(math.e)
NA_QSCALE = NA_HEAD_DIM ** -0.5 * LOG2E
FF_CHUNKS = ((0, 1024), (1024, 2048), (2048, D_FF))
NA_QROWS = 4
NA_KROWS = 12
HY_CTILE = 128
DFT_SPLIT = 64
VMEM_LIMIT = 56 * 1024 * 1024


def _rms(x):
    return x * lax.rsqrt(jnp.mean(x * x, axis=-1, keepdims=True) + EPS)


def _dot(a, b):
    return jnp.dot(a, b, preferred_element_type=F32)


def _swiglu_residual(x, g, wg_ref, wu_ref, wd_ref):
    h = (_rms(x) * g).astype(BF16)
    acc = None
    for lo, hi in FF_CHUNKS:
        gate = _dot(h, wg_ref[:, lo:hi])
        up = _dot(h, wu_ref[:, lo:hi])
        act = (gate * jax.nn.sigmoid(gate) * up).astype(BF16)
        part = _dot(act, wd_ref[lo:hi, :])
        acc = part if acc is None else acc + part
    return x + 0.5 * acc


def _const_spec(shape):
    nd = len(shape)
    return pl.BlockSpec(shape, lambda *_: (0,) * nd, pipeline_mode=pl.Buffered(1))


def _layer_spec(stacked, layer):
    return pl.BlockSpec((None,) + stacked.shape[1:], lambda *_: (layer, 0, 0), pipeline_mode=pl.Buffered(1))


def _row_spec(width):
    return pl.BlockSpec((ROW_TILE, width), lambda i: (i, 0))


def _rows_in_kernel(x_ref, g_ffa_ref, wg_ref, wu_ref, wd_ref, g_mix_ref, w_in_ref,
                    x_out_ref, qkv_ref, hy_ref):
    x = _swiglu_residual(x_ref[...], g_ffa_ref[...], wg_ref, wu_ref, wd_ref)
    x_out_ref[...] = x
    h = (_rms(x) * g_mix_ref[...]).astype(BF16)
    nq = 3 * NA_WIDTH
    qkv = _dot(h, w_in_ref[:, :nq])
    qkv_ref[:, :NA_WIDTH] = (qkv[:, :NA_WIDTH] * NA_QSCALE).astype(BF16)
    qkv_ref[:, NA_WIDTH:] = qkv[:, NA_WIDTH:].astype(BF16)
    hy_ref[...] = _dot(h, w_in_ref[:, nq:])


def _rows_in(x, layer, g_ffa, wg, wu, wd, g_mix, w_in):
    m = x.shape[0]
    return pl.pallas_call(
        _rows_in_kernel,
        grid=(m // ROW_TILE,),
        in_specs=[_row_spec(D_MODEL)] + [_layer_spec(c, layer) for c in (g_ffa, wg, wu, wd, g_mix, w_in)],
        out_specs=[_row_spec(D_MODEL), _row_spec(3 * NA_WIDTH), _row_spec(3 * HY_WIDTH)],
        out_shape=[jax.ShapeDtypeStruct((m, D_MODEL), F32),
                   jax.ShapeDtypeStruct((m, 3 * NA_WIDTH), BF16),
                   jax.ShapeDtypeStruct((m, 3 * HY_WIDTH), F32)],
        compiler_params=pltpu.CompilerParams(dimension_semantics=("parallel",), vmem_limit_bytes=VMEM_LIMIT),
        name="rows_in",
    )(x, g_ffa, wg, wu, wd, g_mix, w_in)


def _rows_out_kernel(x_ref, yna_ref, yhy_ref, p_ref, g_out_ref, w_out_ref, g_ffb_ref, wg_ref, wu_ref, wd_ref,
                     g_ple_ref, w_pg_ref, w_pp_ref, g_fin_ref, o_ref, *, final):
    g_out = g_out_ref[...]
    y = jnp.concatenate([(yna_ref[...] * g_out[:, :NA_WIDTH]).astype(BF16),
                         (_rms(yhy_ref[...]) * g_out[:, NA_WIDTH:]).astype(BF16)], axis=-1)
    x = x_ref[...] + _dot(y, w_out_ref[...])
    x = _swiglu_residual(x, g_ffb_ref[...], wg_ref, wu_ref, wd_ref)
    h = (_rms(x) * g_ple_ref[...]).astype(BF16)
    gate = jax.nn.sigmoid(_dot(h, w_pg_ref[...]))
    x = x + gate * _dot(p_ref[...].astype(BF16), w_pp_ref[...])
    if final:
        x = _rms(x) * g_fin_ref[...]
    o_ref[...] = x


def _rows_out(x, yna, yhy, p, layer, g_out, w_out, g_ffb, wg, wu, wd, g_ple, w_pg, w_pp, g_fin, final):
    m = x.shape[0]
    stacks = (g_out, w_out, g_ffb, wg, wu, wd, g_ple, w_pg, w_pp)
    return pl.pallas_call(
        functools.partial(_rows_out_kernel, final=final),
        grid=(m // ROW_TILE,),
        in_specs=[_row_spec(D_MODEL), _row_spec(NA_WIDTH), _row_spec(HY_WIDTH),
                  pl.BlockSpec((None, ROW_TILE, PLE_DIM), lambda i: (layer, i, 0))]
                 + [_layer_spec(c, layer) for c in stacks] + [_const_spec(g_fin.shape)],
        out_specs=_row_spec(D_MODEL),
        out_shape=jax.ShapeDtypeStruct((m, D_MODEL), F32),
        compiler_params=pltpu.CompilerParams(dimension_semantics=("parallel",), vmem_limit_bytes=VMEM_LIMIT),
        name="rows_out",
    )(x, yna, yhy, p, *stacks, g_fin)


def _na_key_base(g, rows):
    return jnp.clip(NA_QROWS * g - NA_KH // 2, 0, rows - NA_KROWS)


def _bias_assemble_kernel(ext_ref, o_ref, *, index):
    for p, per_q in enumerate(index):
        for q, per_k in enumerate(per_q):
            for k, block in enumerate(per_k):
                o_ref[0, p, 0, q * GRID_W:(q + 1) * GRID_W, k * GRID_W:(k + 1) * GRID_W] = ext_ref[0, 0, block]


def _na_bias_tables(rpb, rows):
    depth = rpb.shape[0]
    n_groups = rows // NA_QROWS
    n_off = 2 * NA_KH - 1
    qc = np.arange(GRID_W)
    cs = np.clip(qc - NA_KW // 2, 0, GRID_W - NA_KW)
    kc = np.arange(GRID_W)
    col_ok = (kc[None, :] >= cs[:, None]) & (kc[None, :] < cs[:, None] + NA_KW)
    dc_idx = np.clip(kc[None, :] - qc[:, None] + NA_KW - 1, 0, 2 * NA_KW - 2)
    dc_hot = jnp.asarray(np.eye(2 * NA_KW - 1, dtype=np.float32)[dc_idx])
    blocks = jnp.einsum("lhab,qcb->lhaqc", rpb, dc_hot, precision=lax.Precision.HIGHEST) * LOG2E
    blocks = jnp.where(jnp.asarray(col_ok)[None, None, None], blocks, -jnp.inf)
    ext = jnp.concatenate([blocks, jnp.full((depth, NA_HEADS, 1, GRID_W, GRID_W), -jnp.inf, F32)], axis=2)
    index = []
    for g in (0, 1, n_groups - 1):
        r = NA_QROWS * g + np.arange(NA_QROWS)
        base = int(np.clip(NA_QROWS * g - NA_KH // 2, 0, rows - NA_KROWS))
        krow = base + np.arange(NA_KROWS)
        rs = np.clip(r - NA_KH // 2, 0, rows - NA_KH)
        row_ok = (krow[None, :] >= rs[:, None]) & (krow[None, :] < rs[:, None] + NA_KH)
        dr_idx = krow[None, :] - r[:, None] + NA_KH - 1
        index.append(tuple(tuple(int(dr_idx[q, k]) if row_ok[q, k] else n_off for k in range(NA_KROWS))
                           for q in range(NA_QROWS)))
    tq, tk = NA_QROWS * GRID_W, NA_KROWS * GRID_W
    return pl.pallas_call(
        functools.partial(_bias_assemble_kernel, index=tuple(index)),
        grid=(depth, NA_HEADS),
        in_specs=[pl.BlockSpec((1, 1, n_off + 1, GRID_W, GRID_W), lambda l, h: (l, h, 0, 0, 0))],
        out_specs=pl.BlockSpec((1, len(index), 1, tq, tk), lambda l, h: (l, 0, h, 0, 0)),
        out_shape=jax.ShapeDtypeStruct((depth, len(index), NA_HEADS, tq, tk), F32),
        compiler_params=pltpu.CompilerParams(dimension_semantics=("parallel", "parallel")),
        name="na_bias",
    )(ext)


def _na_kernel(q_ref, k_ref, v_ref, bias_ref, o_ref, *, rows):
    g = pl.program_id(1)
    start = pl.multiple_of(_na_key_base(g, rows) * GRID_W, GRID_W)
    nk = NA_KROWS * GRID_W
    tq = NA_QROWS * GRID_W
    lo_half = lax.broadcasted_iota(jnp.int32, (1, LANES), 1) < NA_HEAD_DIM
    outs = []
    for pair in range(NA_HEADS // 2):
        cols = slice(pair * LANES, (pair + 1) * LANES)
        qp = q_ref[0, :, cols]
        kp = k_ref[0, pl.ds(start, nk), cols]
        vp = v_ref[0, pl.ds(start, nk), cols]
        q2 = jnp.concatenate([jnp.where(lo_half, qp, 0), jnp.where(lo_half, 0, qp)], axis=0)
        s = lax.dot_general(q2, kp, (((1,), (1,)), ((), ())), preferred_element_type=F32)
        s = s + bias_ref[0, 2 * pair:2 * pair + 2].reshape(2 * tq, nk)
        e = jnp.exp2(s - jnp.max(s, axis=-1, keepdims=True))
        inv = 1.0 / jnp.sum(e, axis=-1, keepdims=True)
        e = e.astype(BF16)
        e2 = jnp.concatenate([e[:tq], e[tq:]], axis=1)
        v2 = jnp.concatenate([jnp.where(lo_half, vp, 0), jnp.where(lo_half, 0, vp)], axis=0)
        outs.append(_dot(e2, v2) * jnp.where(lo_half, inv[:tq], inv[tq:]))
    o_ref[0] = _rms(jnp.concatenate(outs, axis=-1))


def _na(qkv, bias, layer, batch, seq):
    rows = seq // GRID_W
    n_groups = rows // NA_QROWS
    tq = NA_QROWS * GRID_W
    tk = NA_KROWS * GRID_W

    def pattern(b, g):
        return (layer, jnp.where(g == 0, 0, jnp.where(g == n_groups - 1, 2, 1)), 0, 0, 0)

    return pl.pallas_call(
        functools.partial(_na_kernel, rows=rows),
        grid=(batch, n_groups),
        in_specs=[pl.BlockSpec((1, tq, NA_WIDTH), lambda b, g: (b, g, 0)),
                  pl.BlockSpec((1, seq, NA_WIDTH), lambda b, g: (b, 0, 1)),
                  pl.BlockSpec((1, seq, NA_WIDTH), lambda b, g: (b, 0, 2)),
                  pl.BlockSpec((None, 1, NA_HEADS, tq, tk), pattern)],
        out_specs=pl.BlockSpec((1, tq, NA_WIDTH), lambda b, g: (b, g, 0)),
        out_shape=jax.ShapeDtypeStruct((batch, seq, NA_WIDTH), F32),
        compiler_params=pltpu.CompilerParams(dimension_semantics=("parallel", "arbitrary"),
                                             vmem_limit_bytes=VMEM_LIMIT),
        name="na",
    )(qkv, qkv, qkv, bias)


def _dft_mats(seq):
    half = seq // 2
    m = jnp.arange(half, dtype=jnp.int32)[None, :]
    a = jnp.arange(half // DFT_SPLIT, dtype=jnp.int32)[:, None]
    b = jnp.arange(DFT_SPLIT, dtype=jnp.int32)[:, None]
    ang_a = ((DFT_SPLIT * a * m) % seq).astype(F32) * (2.0 * math.pi / seq)
    ang_b = ((b * m) % seq).astype(F32) * (2.0 * math.pi / seq)
    ca, sa = jnp.cos(ang_a)[:, None, :], jnp.sin(ang_a)[:, None, :]
    cb, sb = jnp.cos(ang_b)[None, :, :], jnp.sin(ang_b)[None, :, :]
    cmat = (ca * cb - sa * sb).reshape(half, half)
    smat = (sa * cb + ca * sb).reshape(half, half)
    ang_t = jnp.arange(half, dtype=F32)[:, None] * (math.pi / seq)
    return cmat.astype(BF16), smat.astype(BF16), jnp.cos(ang_t), jnp.sin(ang_t)


def _filter_features(seq):
    bands = (FILTER_EMB - 1) // 2
    t = jnp.linspace(0.0, 1.0, seq, dtype=F32)[:, None]
    w = 2.0 * math.pi * jnp.arange(seq, dtype=F32)[:, None] / seq
    f = jnp.linspace(1e-4, bands - 1, bands, dtype=F32)[None, :]
    z = jnp.concatenate([t, jnp.cos(f * w), -jnp.sin(f * w)], axis=-1)
    return t, jnp.pad(z, ((0, 0), (0, LANES - FILTER_EMB)))


def _half_spectra(xe, xo, c_ref, s_ref, ct, st):
    nc = xe.shape[1]
    x = jnp.concatenate([xe, xo], axis=-1)
    rc = _dot(c_ref[...], x)
    rs = _dot(s_ref[...], x)
    ae, ao = rc[:, :nc], rc[:, nc:]
    be, bo = rs[:, :nc], rs[:, nc:]
    tr = ct * ao - st * bo
    ti = ct * bo + st * ao
    return ae + tr, -(be + ti), ae - tr, ti - be


def _alternating(n):
    row = lax.broadcasted_iota(jnp.int32, (n, 1), 0)
    return jnp.where((row & 1) == 0, 1.0, -1.0)


def _filt_kernel(z_ref, w1_ref, b1_ref, w2_ref, b2_ref, w3_ref, b3_ref, w4_ref, fr_ref, t_ref,
                 dl_ref, c_ref, s_ref, ct_ref, st_ref, kpr_ref, kpi_ref, kqr_ref, kqi_ref, kmid_ref):
    half = z_ref.shape[0] // 2
    n_fft = 4 * half
    hp = lax.Precision.HIGHEST
    fr = fr_ref[...]
    h = jnp.sin(fr * (jnp.dot(z_ref[...], w1_ref[...], precision=hp, preferred_element_type=F32) + b1_ref[...]))
    h = jnp.sin(fr * (jnp.dot(h, w2_ref[...], precision=hp, preferred_element_type=F32) + b2_ref[...]))
    h = jnp.sin(fr * (jnp.dot(h, w3_ref[...], precision=hp, preferred_element_type=F32) + b3_ref[...]))
    h = jnp.dot(h, w4_ref[...], precision=hp, preferred_element_type=F32)
    decay = jnp.exp(-t_ref[...] * dl_ref[...])
    h = h * jnp.concatenate([decay, decay], axis=-1)
    he, ho = h[:half], h[half:]
    first = lax.broadcasted_iota(jnp.int32, (half, 1), 0) == 0
    hf0, hb0 = he[:, :HY_WIDTH], he[:, HY_WIDTH:]
    tot = jnp.sum(jnp.abs(he) + jnp.abs(ho), axis=0, keepdims=True)
    lag0 = jnp.sum(jnp.where(first, jnp.abs(hf0 + hb0) - jnp.abs(hf0) - jnp.abs(hb0), 0.0), axis=0, keepdims=True)
    inv = 1.0 / (tot[:, :HY_WIDTH] + tot[:, HY_WIDTH:] + lag0)

    ct, st = ct_ref[...], st_ref[...]
    fpr, fpi, fqr, fqi = _half_spectra(he[:, :HY_WIDTH].astype(BF16), ho[:, :HY_WIDTH].astype(BF16),
                                       c_ref, s_ref, ct, st)
    bpr, bpi, bqr, bqi = _half_spectra(he[:, HY_WIDTH:].astype(BF16), ho[:, HY_WIDTH:].astype(BF16),
                                       c_ref, s_ref, ct, st)
    scale = inv * jnp.where(first, 1.0 / n_fft, 2.0 / n_fft)
    kpr_ref[...] = (fpr + bpr) * scale
    kpi_ref[...] = (fpi - bpi) * scale
    kqr_ref[...] = (fqr + bqr) * scale
    kqi_ref[...] = (fqi - bqi) * scale
    alt = _alternating(half)
    a1 = jnp.sum(he * alt, axis=0, keepdims=True)
    b1 = jnp.sum(ho * alt, axis=0, keepdims=True)
    mid = inv * (2.0 / n_fft)
    kmid_ref[0:1, :] = (a1[:, :HY_WIDTH] + a1[:, HY_WIDTH:]) * mid
    kmid_ref[1:2, :] = (b1[:, HY_WIDTH:] - b1[:, :HY_WIDTH]) * mid


def _filter_spectrum(z, t, deltas, cmat, smat, ct, st, w1, b1, w2, b2, w3, b3, w4, freq):
    half = z.shape[0] // 2
    pad2 = lambda w, r, c: jnp.pad(w, ((0, r - w.shape[0]), (0, c - w.shape[1])))
    even_then_odd = lambda a: jnp.concatenate([a[0::2], a[1::2]], axis=0)
    args = (even_then_odd(z), pad2(w1, LANES, LANES), pad2(b1[None], 1, LANES), pad2(w2, LANES, LANES),
            pad2(b2[None], 1, LANES), pad2(w3, LANES, LANES), pad2(b3[None], 1, LANES),
            pad2(w4, LANES, 2 * HY_WIDTH), pad2(freq[None], 1, LANES), even_then_odd(t), deltas, cmat, smat, ct, st)
    spec = jax.ShapeDtypeStruct((half, HY_WIDTH), F32)
    return pl.pallas_call(
        _filt_kernel,
        out_shape=[spec, spec, spec, spec, jax.ShapeDtypeStruct((2, HY_WIDTH), F32)],
        compiler_params=pltpu.CompilerParams(vmem_limit_bytes=VMEM_LIMIT),
        name="filt",
    )(*args)


def _short_conv(u_ref, w_ref, b_ref):
    half = u_ref.shape[0] // 2
    row = lax.broadcasted_iota(jnp.int32, (half, 1), 0)
    ue = u_ref[pl.ds(0, half, stride=2), :]
    uo = u_ref[pl.ds(1, half, stride=2), :]
    uo_prev = jnp.where(row == 0, 0.0, pltpu.roll(uo, 1, 0))
    ue_next = jnp.where(row == half - 1, 0.0, pltpu.roll(ue, half - 1, 0))
    w0, w1, w2, b = w_ref[0:1, :], w_ref[1:2, :], w_ref[2:3, :], b_ref[...]
    return uo_prev * w0 + ue * w1 + uo * w2 + b, ue * w0 + uo * w1 + ue_next * w2 + b


def _hyena_kernel(x0_ref, x1_ref, hv_ref, w0_ref, w1_ref, wv_ref, b0_ref, b1_ref, bv_ref,
                  c_ref, s_ref, ct_ref, st_ref, kpr_ref, kpi_ref, kqr_ref, kqi_ref, kmid_ref, hb_ref, o_ref):
    half = x0_ref.shape[0] // 2
    nc = x0_ref.shape[1]
    ct, st = ct_ref[...], st_ref[...]
    hve, hvo = _short_conv(hv_ref, wv_ref, bv_ref)
    x1e, x1o = _short_conv(x1_ref, w1_ref, b1_ref)
    ve, vo = hve * x1e, hvo * x1o
    pr, pi, qr, qi = _half_spectra(ve.astype(BF16), vo.astype(BF16), c_ref, s_ref, ct, st)
    kpr, kpi, kqr, kqi = kpr_ref[...], kpi_ref[...], kqr_ref[...], kqi_ref[...]
    ypr, ypi = pr * kpr - pi * kpi, pr * kpi + pi * kpr
    yqr, yqi = qr * kqr - qi * kqi, qr * kqi + qi * kqr
    dr, di = ypr - yqr, ypi - yqi
    zr = jnp.concatenate([(ypr + yqr).astype(BF16), (dr * ct - di * st).astype(BF16)], axis=-1)
    zi = jnp.concatenate([(ypi + yqi).astype(BF16), (dr * st + di * ct).astype(BF16)], axis=-1)
    y = _dot(c_ref[...], zr) - _dot(s_ref[...], zi)
    alt = _alternating(half)
    a1 = jnp.sum(ve * alt, axis=0, keepdims=True)
    b1 = jnp.sum(vo * alt, axis=0, keepdims=True)
    kmr, kmi = kmid_ref[0:1, :], kmid_ref[1:2, :]
    ye = y[:, :nc] + alt * (a1 * kmr + b1 * kmi)
    yo = y[:, nc:] - alt * (a1 * kmi - b1 * kmr)
    x0e, x0o = _short_conv(x0_ref, w0_ref, b0_ref)
    hb = hb_ref[...]
    o_ref[pl.ds(0, half, stride=2), :] = x0e * (ye + ve * hb)
    o_ref[pl.ds(1, half, stride=2), :] = x0o * (yo + vo * hb)


def _hyena(hy, w_sc, b_sc, cmat, smat, ct, st, kspec, hy_bias, batch, seq):
    kpr, kpi, kqr, kqi, kmid = kspec
    nt = HY_WIDTH // HY_CTILE
    half = seq // 2
    blk = lambda off: pl.BlockSpec((None, seq, HY_CTILE), lambda j, b: (b, 0, off * nt + j))
    wblk = lambda off: pl.BlockSpec((3, HY_CTILE), lambda j, b: (0, off * nt + j))
    bblk = lambda off: pl.BlockSpec((1, HY_CTILE), lambda j, b: (0, off * nt + j))
    chan = lambda r: pl.BlockSpec((r, HY_CTILE), lambda j, b: (0, j))
    spec = pl.BlockSpec((half, HY_CTILE), lambda j, b: (0, j), pipeline_mode=pl.Buffered(1))
    return pl.pallas_call(
        _hyena_kernel,
        grid=(nt, batch),
        in_specs=[blk(0), blk(1), blk(2), wblk(0), wblk(1), wblk(2), bblk(0), bblk(1), bblk(2),
                  _const_spec(cmat.shape), _const_spec(smat.shape), _const_spec(ct.shape), _const_spec(st.shape),
                  spec, spec, spec, spec, chan(2), chan(1)],
        out_specs=pl.BlockSpec((None, seq, HY_CTILE), lambda j, b: (b, 0, j)),
        out_shape=jax.ShapeDtypeStruct((batch, seq, HY_WIDTH), F32),
        compiler_params=pltpu.CompilerParams(dimension_semantics=("arbitrary", "arbitrary"),
                                             vmem_limit_bytes=VMEM_LIMIT),
        name="hyena",
    )(hy, hy, hy, w_sc, w_sc, w_sc, b_sc, b_sc, b_sc, cmat, smat, ct, st, kpr, kpi, kqr, kqi, kmid, hy_bias)


def kernel(x, p, g_ffa, w_ffa_gate, w_ffa_up, w_ffa_down, g_mix, w_in, na_rpb, w_sc, b_sc, w_f1, b_f1, w_f2, b_f2, w_f3, b_f3, w_f4, filt_freq, hy_bias, g_out, w_out, g_ffb, w_ffb_gate, w_ffb_up, w_ffb_down, g_ple, w_ple_gate, w_ple_proj, g_final):
    batch, seq, _ = x.shape
    depth = p.shape[0]
    m = batch * seq
    rows = seq // GRID_W
    bf = lambda w: w.astype(BF16)
    vec = lambda g: g[:, None, :]

    cmat, smat, ct, st = _dft_mats(seq)
    t, z = _filter_features(seq)
    deltas = jnp.abs(jnp.linspace(MIN_DECAY, MAX_DECAY, HY_WIDTH, dtype=F32))[None, :]

    in_params = (vec(g_ffa), bf(w_ffa_gate), bf(w_ffa_up), bf(w_ffa_down), vec(g_mix), bf(w_in))
    out_params = (vec(g_out), bf(w_out), vec(g_ffb), bf(w_ffb_gate), bf(w_ffb_up), bf(w_ffb_down),
                  vec(g_ple), bf(w_ple_gate), bf(w_ple_proj))
    pr = p.reshape(depth, m, PLE_DIM)

    bias = _na_bias_tables(na_rpb, rows)

    xr = x.reshape(m, D_MODEL)
    for i in range(depth):
        xr, qkv, hy = _rows_in(xr, i, *in_params)
        yna = _na(qkv.reshape(batch, seq, 3 * NA_WIDTH), bias, i, batch, seq)
        kspec = _filter_spectrum(z, t, deltas, cmat, smat, ct, st, w_f1[i], b_f1[i], w_f2[i], b_f2[i],
                                 w_f3[i], b_f3[i], w_f4[i], filt_freq[i])
        yhy = _hyena(hy.reshape(batch, seq, 3 * HY_WIDTH), w_sc[i], b_sc[i][None], cmat, smat, ct, st, kspec,
                     hy_bias[i][None], batch, seq)
        xr = _rows_out(xr, yna.reshape(m, NA_WIDTH), yhy.reshape(m, HY_WIDTH), pr, i, *out_params,
                       g_final[None], final=(i == depth - 1))
    return xr.reshape(batch, seq, D_MODEL)
```

```python
import functools
import math

import numpy as np
import jax
import jax.numpy as jnp
from jax import lax
from jax.experimental import pallas as pl
from jax.experimental.pallas import tpu as pltpu

F32 = jnp.float32
BF16 = jnp.bfloat16

D_MODEL = 1024
GRID_W = 64
NA_HEADS = 8
NA_HEAD_DIM = 64
NA_WIDTH = NA_HEADS * NA_HEAD_DIM
NA_KH = 8
NA_KW = 16
HY_WIDTH = D_MODEL - NA_WIDTH
FILTER_EMB = 33
DECAY_TARGET = 1e-2
FAST_DECAY_PCT = 0.3
SLOW_DECAY_PCT = 1.5
MAX_DECAY = math.log(DECAY_TARGET) / FAST_DECAY_PCT
MIN_DECAY = math.log(DECAY_TARGET) / SLOW_DECAY_PCT
D_FF = 2816
PLE_DIM = 256
EPS = 1e-6

LANES = 128
ROW_TILE = 512
LOG2E = math.log2(math.e)
NA_QSCALE = NA_HEAD_DIM ** -0.5 * LOG2E
FF_CHUNKS = ((0, 1024), (1024, 2048), (2048, D_FF))
NA_QROWS = 4
NA_KROWS = 12
HY_CTILE = 128
HY_SUB = 2
DFT_SPLIT = 64
VMEM_LIMIT = 56 * 1024 * 1024


def _rms(x):
    return x * lax.rsqrt(jnp.mean(x * x, axis=-1, keepdims=True) + EPS)


def _dot(a, b):
    return jnp.dot(a, b, preferred_element_type=F32)


def _swiglu_residual(x, g, wg_ref, wu_ref, wd_ref):
    h = (_rms(x) * g).astype(BF16)
    acc = None
    for lo, hi in FF_CHUNKS:
        gate = _dot(h, wg_ref[:, lo:hi])
        up = _dot(h, wu_ref[:, lo:hi])
        act = (gate * jax.nn.sigmoid(gate) * up).astype(BF16)
        part = _dot(act, wd_ref[lo:hi, :])
        acc = part if acc is None else acc + part
    return x + 0.5 * acc


def _const_spec(shape):
    nd = len(shape)
    return pl.BlockSpec(shape, lambda *_: (0,) * nd, pipeline_mode=pl.Buffered(1))


def _layer_spec(stacked, layer):
    return pl.BlockSpec((None,) + stacked.shape[1:], lambda *_: (layer, 0, 0), pipeline_mode=pl.Buffered(1))


def _row_spec(width):
    return pl.BlockSpec((ROW_TILE, width), lambda i: (i, 0))


def _rows_in_kernel(x_ref, g_ffa_ref, wg_ref, wu_ref, wd_ref, g_mix_ref, w_in_ref,
                    x_out_ref, qkv_ref, hy_ref):
    x = _swiglu_residual(x_ref[...], g_ffa_ref[...], wg_ref, wu_ref, wd_ref)
    x_out_ref[...] = x
    h = (_rms(x) * g_mix_ref[...]).astype(BF16)
    nq = 3 * NA_WIDTH
    qkv = _dot(h, w_in_ref[:, :nq])
    qkv_ref[:, :NA_WIDTH] = (qkv[:, :NA_WIDTH] * NA_QSCALE).astype(BF16)
    qkv_ref[:, NA_WIDTH:] = qkv[:, NA_WIDTH:].astype(BF16)
    hy_ref[...] = _dot(h, w_in_ref[:, nq:])


def _rows_in(x, layer, g_ffa, wg, wu, wd, g_mix, w_in):
    m = x.shape[0]
    return pl.pallas_call(
        _rows_in_kernel,
        grid=(m // ROW_TILE,),
        in_specs=[_row_spec(D_MODEL)] + [_layer_spec(c, layer) for c in (g_ffa, wg, wu, wd, g_mix, w_in)],
        out_specs=[_row_spec(D_MODEL), _row_spec(3 * NA_WIDTH), _row_spec(3 * HY_WIDTH)],
        out_shape=[jax.ShapeDtypeStruct((m, D_MODEL), F32),
                   jax.ShapeDtypeStruct((m, 3 * NA_WIDTH), BF16),
                   jax.ShapeDtypeStruct((m, 3 * HY_WIDTH), F32)],
        compiler_params=pltpu.CompilerParams(dimension_semantics=("parallel",), vmem_limit_bytes=VMEM_LIMIT),
        name="rows_in",
    )(x, g_ffa, wg, wu, wd, g_mix, w_in)


def _rows_out_kernel(x_ref, yna_ref, yhy_ref, p_ref, g_out_ref, w_out_ref, g_ffb_ref, wg_ref, wu_ref, wd_ref,
                     g_ple_ref, w_pg_ref, w_pp_ref, g_fin_ref, o_ref, *, final):
    g_out = g_out_ref[...]
    y = jnp.concatenate([(yna_ref[...] * g_out[:, :NA_WIDTH]).astype(BF16),
                         (_rms(yhy_ref[...]) * g_out[:, NA_WIDTH:]).astype(BF16)], axis=-1)
    x = x_ref[...] + _dot(y, w_out_ref[...])
    x = _swiglu_residual(x, g_ffb_ref[...], wg_ref, wu_ref, wd_ref)
    h = (_rms(x) * g_ple_ref[...]).astype(BF16)
    gate = jax.nn.sigmoid(_dot(h, w_pg_ref[...]))
    x = x + gate * _dot(p_ref[...].astype(BF16), w_pp_ref[...])
    if final:
        x = _rms(x) * g_fin_ref[...]
    o_ref[...] = x


def _rows_out(x, yna, yhy, p, layer, g_out, w_out, g_ffb, wg, wu, wd, g_ple, w_pg, w_pp, g_fin, final):
    m = x.shape[0]
    stacks = (g_out, w_out, g_ffb, wg, wu, wd, g_ple, w_pg, w_pp)
    return pl.pallas_call(
        functools.partial(_rows_out_kernel, final=final),
        grid=(m // ROW_TILE,),
        in_specs=[_row_spec(D_MODEL), _row_spec(NA_WIDTH), _row_spec(HY_WIDTH),
                  pl.BlockSpec((None, ROW_TILE, PLE_DIM), lambda i: (layer, i, 0))]
                 + [_layer_spec(c, layer) for c in stacks] + [_const_spec(g_fin.shape)],
        out_specs=_row_spec(D_MODEL),
        out_shape=jax.ShapeDtypeStruct((m, D_MODEL), F32),
        compiler_params=pltpu.CompilerParams(dimension_semantics=("parallel",), vmem_limit_bytes=VMEM_LIMIT),
        name="rows_out",
    )(x, yna, yhy, p, *stacks, g_fin)


def _na_key_base(g, rows):
    return jnp.clip(NA_QROWS * g - NA_KH // 2, 0, rows - NA_KROWS)


def _bias_assemble_kernel(ext_ref, o_ref, *, index):
    for p, per_q in enumerate(index):
        for q, per_k in enumerate(per_q):
            for k, block in enumerate(per_k):
                o_ref[0, p, 0, q * GRID_W:(q + 1) * GRID_W, k * GRID_W:(k + 1) * GRID_W] = ext_ref[0, 0, block]


def _na_bias_tables(rpb, rows):
    depth = rpb.shape[0]
    n_groups = rows // NA_QROWS
    n_off = 2 * NA_KH - 1
    qc = np.arange(GRID_W)
    cs = np.clip(qc - NA_KW // 2, 0, GRID_W - NA_KW)
    kc = np.arange(GRID_W)
    col_ok = (kc[None, :] >= cs[:, None]) & (kc[None, :] < cs[:, None] + NA_KW)
    dc_idx = np.clip(kc[None, :] - qc[:, None] + NA_KW - 1, 0, 2 * NA_KW - 2)
    dc_hot = jnp.asarray(np.eye(2 * NA_KW - 1, dtype=np.float32)[dc_idx])
    blocks = jnp.einsum("lhab,qcb->lhaqc", rpb, dc_hot, precision=lax.Precision.HIGHEST) * LOG2E
    blocks = jnp.where(jnp.asarray(col_ok)[None, None, None], blocks, -jnp.inf)
    ext = jnp.concatenate([blocks, jnp.full((depth, NA_HEADS, 1, GRID_W, GRID_W), -jnp.inf, F32)], axis=2)
    index = []
    for g in (0, 1, n_groups - 1):
        r = NA_QROWS * g + np.arange(NA_QROWS)
        base = int(np.clip(NA_QROWS * g - NA_KH // 2, 0, rows - NA_KROWS))
        krow = base + np.arange(NA_KROWS)
        rs = np.clip(r - NA_KH // 2, 0, rows - NA_KH)
        row_ok = (krow[None, :] >= rs[:, None]) & (krow[None, :] < rs[:, None] + NA_KH)
        dr_idx = krow[None, :] - r[:, None] + NA_KH - 1
        index.append(tuple(tuple(int(dr_idx[q, k]) if row_ok[q, k] else n_off for k in range(NA_KROWS))
                           for q in range(NA_QROWS)))
    tq, tk = NA_QROWS * GRID_W, NA_KROWS * GRID_W
    return pl.pallas_call(
        functools.partial(_bias_assemble_kernel, index=tuple(index)),
        grid=(depth, NA_HEADS),
        in_specs=[pl.BlockSpec((1, 1, n_off + 1, GRID_W, GRID_W), lambda l, h: (l, h, 0, 0, 0))],
        out_specs=pl.BlockSpec((1, len(index), 1, tq, tk), lambda l, h: (l, 0, h, 0, 0)),
        out_shape=jax.ShapeDtypeStruct((depth, len(index), NA_HEADS, tq, tk), F32),
        compiler_params=pltpu.CompilerParams(dimension_semantics=("parallel", "parallel")),
        name="na_bias",
    )(ext)


def _na_kernel(q_ref, k_ref, v_ref, bias_ref, o_ref, *, rows):
    g = pl.program_id(1)
    start = pl.multiple_of(_na_key_base(g, rows) * GRID_W, GRID_W)
    nk = NA_KROWS * GRID_W
    tq = NA_QROWS * GRID_W
    lo_half = lax.broadcasted_iota(jnp.int32, (1, LANES), 1) < NA_HEAD_DIM
    outs = []
    for pair in range(NA_HEADS // 2):
        cols = slice(pair * LANES, (pair + 1) * LANES)
        qp = q_ref[0, :, cols]
        kp = k_ref[0, pl.ds(start, nk), cols]
        vp = v_ref[0, pl.ds(start, nk), cols]
        q2 = jnp.concatenate([jnp.where(lo_half, qp, 0), jnp.where(lo_half, 0, qp)], axis=0)
        s = lax.dot_general(q2, kp, (((1,), (1,)), ((), ())), preferred_element_type=F32)
        s = s + bias_ref[0, 2 * pair:2 * pair + 2].reshape(2 * tq, nk)
        e = jnp.exp2(s - jnp.max(s, axis=-1, keepdims=True))
        inv = 1.0 / jnp.sum(e, axis=-1, keepdims=True)
        e = e.astype(BF16)
        e2 = jnp.concatenate([e[:tq], e[tq:]], axis=1)
        v2 = jnp.concatenate([jnp.where(lo_half, vp, 0), jnp.where(lo_half, 0, vp)], axis=0)
        outs.append(_dot(e2, v2) * jnp.where(lo_half, inv[:tq], inv[tq:]))
    o_ref[0] = _rms(jnp.concatenate(outs, axis=-1))


def _na(qkv, bias, layer, batch, seq):
    rows = seq // GRID_W
    n_groups = rows // NA_QROWS
    tq = NA_QROWS * GRID_W
    tk = NA_KROWS * GRID_W

    def pattern(b, g):
        return (layer, jnp.where(g == 0, 0, jnp.where(g == n_groups - 1, 2, 1)), 0, 0, 0)

    return pl.pallas_call(
        functools.partial(_na_kernel, rows=rows),
        grid=(batch, n_groups),
        in_specs=[pl.BlockSpec((1, tq, NA_WIDTH), lambda b, g: (b, g, 0)),
                  pl.BlockSpec((1, seq, NA_WIDTH), lambda b, g: (b, 0, 1)),
                  pl.BlockSpec((1, seq, NA_WIDTH), lambda b, g: (b, 0, 2)),
                  pl.BlockSpec((None, 1, NA_HEADS, tq, tk), pattern)],
        out_specs=pl.BlockSpec((1, tq, NA_WIDTH), lambda b, g: (b, g, 0)),
        out_shape=jax.ShapeDtypeStruct((batch, seq, NA_WIDTH), F32),
        compiler_params=pltpu.CompilerParams(dimension_semantics=("parallel", "arbitrary"),
                                             vmem_limit_bytes=VMEM_LIMIT),
        name="na",
    )(qkv, qkv, qkv, bias)


def _dft_mats(seq):
    half = seq // 2
    m = jnp.arange(half, dtype=jnp.int32)[None, :]
    a = jnp.arange(half // DFT_SPLIT, dtype=jnp.int32)[:, None]
    b = jnp.arange(DFT_SPLIT, dtype=jnp.int32)[:, None]
    ang_a = ((DFT_SPLIT * a * m) % seq).astype(F32) * (2.0 * math.pi / seq)
    ang_b = ((b * m) % seq).astype(F32) * (2.0 * math.pi / seq)
    ca, sa = jnp.cos(ang_a)[:, None, :], jnp.sin(ang_a)[:, None, :]
    cb, sb = jnp.cos(ang_b)[None, :, :], jnp.sin(ang_b)[None, :, :]
    cmat = (ca * cb - sa * sb).reshape(half, half)
    smat = (sa * cb + ca * sb).reshape(half, half)
    ang_t = jnp.arange(half, dtype=F32)[:, None] * (math.pi / seq)
    return cmat.astype(BF16), smat.astype(BF16), jnp.cos(ang_t), jnp.sin(ang_t)


def _filter_features(seq):
    bands = (FILTER_EMB - 1) // 2
    t = jnp.linspace(0.0, 1.0, seq, dtype=F32)[:, None]
    w = 2.0 * math.pi * jnp.arange(seq, dtype=F32)[:, None] / seq
    f = jnp.linspace(1e-4, bands - 1, bands, dtype=F32)[None, :]
    z = jnp.concatenate([t, jnp.cos(f * w), -jnp.sin(f * w)], axis=-1)
    return t, jnp.pad(z, ((0, 0), (0, LANES - FILTER_EMB)))


def _half_spectra(xe, xo, c_ref, s_ref, ct, st):
    nc = xe.shape[1]
    x = jnp.concatenate([xe, xo], axis=-1)
    rc = _dot(c_ref[...], x)
    rs = _dot(s_ref[...], x)
    ae, ao = rc[:, :nc], rc[:, nc:]
    be, bo = rs[:, :nc], rs[:, nc:]
    tr = ct * ao - st * bo
    ti = ct * bo + st * ao
    return ae + tr, -(be + ti), ae - tr, ti - be


def _alternating(n):
    row = lax.broadcasted_iota(jnp.int32, (n, 1), 0)
    return jnp.where((row & 1) == 0, 1.0, -1.0)


def _filt_kernel(z_ref, w1_ref, b1_ref, w2_ref, b2_ref, w3_ref, b3_ref, w4_ref, fr_ref, t_ref,
                 dl_ref, c_ref, s_ref, ct_ref, st_ref, kpr_ref, kpi_ref, kqr_ref, kqi_ref, kmid_ref):
    half = z_ref.shape[0] // 2
    n_fft = 4 * half
    hp = lax.Precision.HIGHEST
    fr = fr_ref[...]
    h = jnp.sin(fr * (jnp.dot(z_ref[...], w1_ref[...], precision=hp, preferred_element_type=F32) + b1_ref[...]))
    h = jnp.sin(fr * (jnp.dot(h, w2_ref[...], precision=hp, preferred_element_type=F32) + b2_ref[...]))
    h = jnp.sin(fr * (jnp.dot(h, w3_ref[...], precision=hp, preferred_element_type=F32) + b3_ref[...]))
    h = jnp.dot(h, w4_ref[...], precision=hp, preferred_element_type=F32)
    decay = jnp.exp(-t_ref[...] * dl_ref[...])
    h = h * jnp.concatenate([decay, decay], axis=-1)
    he, ho = h[:half], h[half:]
    first = lax.broadcasted_iota(jnp.int32, (half, 1), 0) == 0
    hf0, hb0 = he[:, :HY_WIDTH], he[:, HY_WIDTH:]
    tot = jnp.sum(jnp.abs(he) + jnp.abs(ho), axis=0, keepdims=True)
    lag0 = jnp.sum(jnp.where(first, jnp.abs(hf0 + hb0) - jnp.abs(hf0) - jnp.abs(hb0), 0.0), axis=0, keepdims=True)
    inv = 1.0 / (tot[:, :HY_WIDTH] + tot[:, HY_WIDTH:] + lag0)

    ct, st = ct_ref[...], st_ref[...]
    fpr, fpi, fqr, fqi = _half_spectra(he[:, :HY_WIDTH].astype(BF16), ho[:, :HY_WIDTH].astype(BF16),
                                       c_ref, s_ref, ct, st)
    bpr, bpi, bqr, bqi = _half_spectra(he[:, HY_WIDTH:].astype(BF16), ho[:, HY_WIDTH:].astype(BF16),
                                       c_ref, s_ref, ct, st)
    scale = inv * jnp.where(first, 1.0 / n_fft, 2.0 / n_fft)
    kpr_ref[...] = (fpr + bpr) * scale
    kpi_ref[...] = (fpi - bpi) * scale
    kqr_ref[...] = (fqr + bqr) * scale
    kqi_ref[...] = (fqi - bqi) * scale
    alt = _alternating(half)
    a1 = jnp.sum(he * alt, axis=0, keepdims=True)
    b1 = jnp.sum(ho * alt, axis=0, keepdims=True)
    mid = inv * (2.0 / n_fft)
    kmid_ref[0:1, :] = (a1[:, :HY_WIDTH] + a1[:, HY_WIDTH:]) * mid
    kmid_ref[1:2, :] = (b1[:, HY_WIDTH:] - b1[:, :HY_WIDTH]) * mid


def _filter_spectrum(z, t, deltas, cmat, smat, ct, st, w1, b1, w2, b2, w3, b3, w4, freq):
    half = z.shape[0] // 2
    pad2 = lambda w, r, c: jnp.pad(w, ((0, r - w.shape[0]), (0, c - w.shape[1])))
    even_then_odd = lambda a: jnp.concatenate([a[0::2], a[1::2]], axis=0)
    args = (even_then_odd(z), pad2(w1, LANES, LANES), pad2(b1[None], 1, LANES), pad2(w2, LANES, LANES),
            pad2(b2[None], 1, LANES), pad2(w3, LANES, LANES), pad2(b3[None], 1, LANES),
            pad2(w4, LANES, 2 * HY_WIDTH), pad2(freq[None], 1, LANES), even_then_odd(t), deltas, cmat, smat, ct, st)
    spec = jax.ShapeDtypeStruct((half, HY_WIDTH), F32)
    return pl.pallas_call(
        _filt_kernel,
        out_shape=[spec, spec, spec, spec, jax.ShapeDtypeStruct((2, HY_WIDTH), F32)],
        compiler_params=pltpu.CompilerParams(vmem_limit_bytes=VMEM_LIMIT),
        name="filt",
    )(*args)


def _short_conv(u_ref, w_ref, b_ref):
    half = u_ref.shape[0] // 2
    row = lax.broadcasted_iota(jnp.int32, (half, 1), 0)
    ue = u_ref[pl.ds(0, half, stride=2), :]
    uo = u_ref[pl.ds(1, half, stride=2), :]
    uo_prev = jnp.where(row == 0, 0.0, pltpu.roll(uo, 1, 0))
    ue_next = jnp.where(row == half - 1, 0.0, pltpu.roll(ue, half - 1, 0))
    w0, w1, w2, b = w_ref[0:1, :], w_ref[1:2, :], w_ref[2:3, :], b_ref[...]
    return uo_prev * w0 + ue * w1 + uo * w2 + b, ue * w0 + uo * w1 + ue_next * w2 + b


def _hyena_tile(x0_ref, x1_ref, hv_ref, w0_ref, w1_ref, wv_ref, b0_ref, b1_ref, bv_ref,
                kpr_ref, kpi_ref, kqr_ref, kqi_ref, kmid_ref, hb_ref, c_ref, s_ref, ct_ref, st_ref, o_ref):
    half = x0_ref.shape[0] // 2
    nc = x0_ref.shape[1]
    ct, st = ct_ref[...], st_ref[...]
    hve, hvo = _short_conv(hv_ref, wv_ref, bv_ref)
    x1e, x1o = _short_conv(x1_ref, w1_ref, b1_ref)
    ve, vo = hve * x1e, hvo * x1o
    pr, pi, qr, qi = _half_spectra(ve.astype(BF16), vo.astype(BF16), c_ref, s_ref, ct, st)
    kpr, kpi, kqr, kqi = kpr_ref[...], kpi_ref[...], kqr_ref[...], kqi_ref[...]
    ypr, ypi = pr * kpr - pi * kpi, pr * kpi + pi * kpr
    yqr, yqi = qr * kqr - qi * kqi, qr * kqi + qi * kqr
    dr, di = ypr - yqr, ypi - yqi
    zr = jnp.concatenate([(ypr + yqr).astype(BF16), (dr * ct - di * st).astype(BF16)], axis=-1)
    zi = jnp.concatenate([(ypi + yqi).astype(BF16), (dr * st + di * ct).astype(BF16)], axis=-1)
    y = _dot(c_ref[...], zr) - _dot(s_ref[...], zi)
    alt = _alternating(half)
    a1 = jnp.sum(ve * alt, axis=0, keepdims=True)
    b1 = jnp.sum(vo * alt, axis=0, keepdims=True)
    kmr, kmi = kmid_ref[0:1, :], kmid_ref[1:2, :]
    ye = y[:, :nc] + alt * (a1 * kmr + b1 * kmi)
    yo = y[:, nc:] - alt * (a1 * kmi - b1 * kmr)
    x0e, x0o = _short_conv(x0_ref, w0_ref, b0_ref)
    hb = hb_ref[...]
    o_ref[pl.ds(0, half, stride=2), :] = x0e * (ye + ve * hb)
    o_ref[pl.ds(1, half, stride=2), :] = x0o * (yo + vo * hb)


HY_TILE_ARGS = 15


def _hyena_kernel(*refs):
    n = HY_SUB * HY_TILE_ARGS
    shared, o_ref, tile_out = refs[n:n + 4], refs[n + 4], refs[n + 5:]
    for t in range(HY_SUB):
        _hyena_tile(*refs[t * HY_TILE_ARGS:(t + 1) * HY_TILE_ARGS], *shared, tile_out[t])
        o_ref[:, t * HY_CTILE:(t + 1) * HY_CTILE] = tile_out[t][...]


def _hyena(hy, w_sc, b_sc, cmat, smat, ct, st, kspec, hy_bias, batch, seq):
    kpr, kpi, kqr, kqi, kmid = kspec
    nt = HY_WIDTH // HY_CTILE
    half = seq // 2
    in_specs, args = [], []
    for t in range(HY_SUB):
        def tile_spec(shape, off, mode=None, t=t):
            lead = (lambda b: (b, 0)) if len(shape) == 3 else (lambda b: (0,))
            return pl.BlockSpec(shape, lambda j, b: lead(b) + (off * nt + j * HY_SUB + t,), pipeline_mode=mode)

        data = lambda off: tile_spec((None, seq, HY_CTILE), off)
        rows_of = lambda r, off=0: tile_spec((r, HY_CTILE), off)
        spec = tile_spec((half, HY_CTILE), 0, pl.Buffered(1))
        in_specs += [data(0), data(1), data(2), rows_of(3, 0), rows_of(3, 1), rows_of(3, 2),
                     rows_of(1, 0), rows_of(1, 1), rows_of(1, 2), spec, spec, spec, spec, rows_of(2), rows_of(1)]
        args += [hy, hy, hy, w_sc, w_sc, w_sc, b_sc, b_sc, b_sc, kpr, kpi, kqr, kqi, kmid, hy_bias]
    in_specs += [_const_spec(cmat.shape), _const_spec(smat.shape), _const_spec(ct.shape), _const_spec(st.shape)]
    args += [cmat, smat, ct, st]
    return pl.pallas_call(
        _hyena_kernel,
        grid=(nt // HY_SUB, batch),
        in_specs=in_specs,
        out_specs=pl.BlockSpec((None, seq, HY_SUB * HY_CTILE), lambda j, b: (b, 0, j)),
        out_shape=jax.ShapeDtypeStruct((batch, seq, HY_WIDTH), F32),
        scratch_shapes=[pltpu.VMEM((seq, HY_CTILE), F32) for _ in range(HY_SUB)],
        compiler_params=pltpu.CompilerParams(dimension_semantics=("arbitrary", "arbitrary"),
                                             vmem_limit_bytes=VMEM_LIMIT),
        name="hyena",
    )(*args)


def kernel(x, p, g_ffa, w_ffa_gate, w_ffa_up, w_ffa_down, g_mix, w_in, na_rpb, w_sc, b_sc, w_f1, b_f1, w_f2, b_f2, w_f3, b_f3, w_f4, filt_freq, hy_bias, g_out, w_out, g_ffb, w_ffb_gate, w_ffb_up, w_ffb_down, g_ple, w_ple_gate, w_ple_proj, g_final):
    batch, seq, _ = x.shape
    depth = p.shape[0]
    m = batch * seq
    rows = seq // GRID_W
    bf = lambda w: w.astype(BF16)
    vec = lambda g: g[:, None, :]

    cmat, smat, ct, st = _dft_mats(seq)
    t, z = _filter_features(seq)
    deltas = jnp.abs(jnp.linspace(MIN_DECAY, MAX_DECAY, HY_WIDTH, dtype=F32))[None, :]

    in_params = (vec(g_ffa), bf(w_ffa_gate), bf(w_ffa_up), bf(w_ffa_down), vec(g_mix), bf(w_in))
    out_params = (vec(g_out), bf(w_out), vec(g_ffb), bf(w_ffb_gate), bf(w_ffb_up), bf(w_ffb_down),
                  vec(g_ple), bf(w_ple_gate), bf(w_ple_proj))
    pr = p.reshape(depth, m, PLE_DIM)

    bias = _na_bias_tables(na_rpb, rows)

    xr = x.reshape(m, D_MODEL)
    for i in range(depth):
        xr, qkv, hy = _rows_in(xr, i, *in_params)
        yna = _na(qkv.reshape(batch, seq, 3 * NA_WIDTH), bias, i, batch, seq)
        kspec = _filter_spectrum(z, t, deltas, cmat, smat, ct, st, w_f1[i], b_f1[i], w_f2[i], b_f2[i],
                                 w_f3[i], b_f3[i], w_f4[i], filt_freq[i])
        yhy = _hyena(hy.reshape(batch, seq, 3 * HY_WIDTH), w_sc[i], b_sc[i][None], cmat, smat, ct, st, kspec,
                     hy_bias[i][None], batch, seq)
        xr = _rows_out(xr, yna.reshape(m, NA_WIDTH), yhy.reshape(m, HY_WIDTH), pr, i, *out_params,
                       g_final[None], final=(i == depth - 1))
    return xr.reshape(batch, seq, D_MODEL)
```

```python
import functools
import math

import numpy as np
import jax
import jax.numpy as jnp
from jax import lax
from jax.experimental import pallas as pl
from jax.experimental.pallas import tpu as pltpu

F32 = jnp.float32
BF16 = jnp.bfloat16

D_MODEL = 1024
GRID_W = 64
NA_HEADS = 8
NA_HEAD_DIM = 64
NA_WIDTH = NA_HEADS * NA_HEAD_DIM
NA_KH = 8
NA_KW = 16
HY_WIDTH = D_MODEL - NA_WIDTH
FILTER_EMB = 33
DECAY_TARGET = 1e-2
FAST_DECAY_PCT = 0.3
SLOW_DECAY_PCT = 1.5
MAX_DECAY = math.log(DECAY_TARGET) / FAST_DECAY_PCT
MIN_DECAY = math.log(DECAY_TARGET) / SLOW_DECAY_PCT
D_FF = 2816
PLE_DIM = 256
EPS = 1e-6

LANES = 128
ROW_TILE = 512
LOG2E = math.log2(math.e)
NA_QSCALE = NA_HEAD_DIM ** -0.5 * LOG2E
FF_CHUNKS = ((0, 1024), (1024, 2048), (2048, D_FF))
NA_QROWS = 4
NA_GROUPS = 2
NA_KROWS = 12
HY_CTILE = 128
HY_SUB = 2
DFT_SPLIT = 64
VMEM_LIMIT = 56 * 1024 * 1024


def _rms(x):
    return x * lax.rsqrt(jnp.mean(x * x, axis=-1, keepdims=True) + EPS)


def _dot(a, b):
    return jnp.dot(a, b, preferred_element_type=F32)


def _swiglu_residual(x, g, wg_ref, wu_ref, wd_ref):
    h = (_rms(x) * g).astype(BF16)
    acc = None
    for lo, hi in FF_CHUNKS:
        gate = _dot(h, wg_ref[:, lo:hi])
        up = _dot(h, wu_ref[:, lo:hi])
        act = (gate * jax.nn.sigmoid(gate) * up).astype(BF16)
        part = _dot(act, wd_ref[lo:hi, :])
        acc = part if acc is None else acc + part
    return x + 0.5 * acc


def _const_spec(shape):
    nd = len(shape)
    return pl.BlockSpec(shape, lambda *_: (0,) * nd, pipeline_mode=pl.Buffered(1))


def _layer_spec(stacked, layer):
    return pl.BlockSpec((None,) + stacked.shape[1:], lambda *_: (layer, 0, 0), pipeline_mode=pl.Buffered(1))


def _row_spec(width):
    return pl.BlockSpec((ROW_TILE, width), lambda i: (i, 0))


def _rows_in_kernel(x_ref, g_ffa_ref, wg_ref, wu_ref, wd_ref, g_mix_ref, w_in_ref,
                    x_out_ref, qkv_ref, hy_ref):
    x = _swiglu_residual(x_ref[...], g_ffa_ref[...], wg_ref, wu_ref, wd_ref)
    x_out_ref[...] = x
    h = (_rms(x) * g_mix_ref[...]).astype(BF16)
    nq = 3 * NA_WIDTH
    qkv = _dot(h, w_in_ref[:, :nq])
    qkv_ref[:, :NA_WIDTH] = (qkv[:, :NA_WIDTH] * NA_QSCALE).astype(BF16)
    qkv_ref[:, NA_WIDTH:] = qkv[:, NA_WIDTH:].astype(BF16)
    hy_ref[...] = _dot(h, w_in_ref[:, nq:])


def _rows_in(x, layer, g_ffa, wg, wu, wd, g_mix, w_in):
    m = x.shape[0]
    return pl.pallas_call(
        _rows_in_kernel,
        grid=(m // ROW_TILE,),
        in_specs=[_row_spec(D_MODEL)] + [_layer_spec(c, layer) for c in (g_ffa, wg, wu, wd, g_mix, w_in)],
        out_specs=[_row_spec(D_MODEL), _row_spec(3 * NA_WIDTH), _row_spec(3 * HY_WIDTH)],
        out_shape=[jax.ShapeDtypeStruct((m, D_MODEL), F32),
                   jax.ShapeDtypeStruct((m, 3 * NA_WIDTH), BF16),
                   jax.ShapeDtypeStruct((m, 3 * HY_WIDTH), F32)],
        compiler_params=pltpu.CompilerParams(dimension_semantics=("parallel",), vmem_limit_bytes=VMEM_LIMIT),
        name="rows_in",
    )(x, g_ffa, wg, wu, wd, g_mix, w_in)


def _rows_out_kernel(x_ref, yna_ref, yhy_ref, p_ref, g_out_ref, w_out_ref, g_ffb_ref, wg_ref, wu_ref, wd_ref,
                     g_ple_ref, w_pg_ref, w_pp_ref, g_fin_ref, o_ref, *, final):
    g_out = g_out_ref[...]
    y = jnp.concatenate([(yna_ref[...] * g_out[:, :NA_WIDTH]).astype(BF16),
                         (_rms(yhy_ref[...]) * g_out[:, NA_WIDTH:]).astype(BF16)], axis=-1)
    x = x_ref[...] + _dot(y, w_out_ref[...])
    x = _swiglu_residual(x, g_ffb_ref[...], wg_ref, wu_ref, wd_ref)
    h = (_rms(x) * g_ple_ref[...]).astype(BF16)
    gate = jax.nn.sigmoid(_dot(h, w_pg_ref[...]))
    x = x + gate * _dot(p_ref[...].astype(BF16), w_pp_ref[...])
    if final:
        x = _rms(x) * g_fin_ref[...]
    o_ref[...] = x


def _rows_out(x, yna, yhy, p, layer, g_out, w_out, g_ffb, wg, wu, wd, g_ple, w_pg, w_pp, g_fin, final):
    m = x.shape[0]
    stacks = (g_out, w_out, g_ffb, wg, wu, wd, g_ple, w_pg, w_pp)
    return pl.pallas_call(
        functools.partial(_rows_out_kernel, final=final),
        grid=(m // ROW_TILE,),
        in_specs=[_row_spec(D_MODEL), _row_spec(NA_WIDTH), _row_spec(HY_WIDTH),
                  pl.BlockSpec((None, ROW_TILE, PLE_DIM), lambda i: (layer, i, 0))]
                 + [_layer_spec(c, layer) for c in stacks] + [_const_spec(g_fin.shape)],
        out_specs=_row_spec(D_MODEL),
        out_shape=jax.ShapeDtypeStruct((m, D_MODEL), F32),
        compiler_params=pltpu.CompilerParams(dimension_semantics=("parallel",), vmem_limit_bytes=VMEM_LIMIT),
        name="rows_out",
    )(x, yna, yhy, p, *stacks, g_fin)


def _na_key_base(g, rows):
    return jnp.clip(NA_QROWS * g - NA_KH // 2, 0, rows - NA_KROWS)


def _bias_assemble_kernel(ext_ref, o_ref, *, index):
    for p, per_q in enumerate(index):
        for q, per_k in enumerate(per_q):
            for k, block in enumerate(per_k):
                o_ref[0, p, 0, q * GRID_W:(q + 1) * GRID_W, k * GRID_W:(k + 1) * GRID_W] = ext_ref[0, 0, block]


def _na_bias_tables(rpb, rows):
    depth = rpb.shape[0]
    n_groups = rows // NA_QROWS
    n_off = 2 * NA_KH - 1
    qc = np.arange(GRID_W)
    cs = np.clip(qc - NA_KW // 2, 0, GRID_W - NA_KW)
    kc = np.arange(GRID_W)
    col_ok = (kc[None, :] >= cs[:, None]) & (kc[None, :] < cs[:, None] + NA_KW)
    dc_idx = np.clip(kc[None, :] - qc[:, None] + NA_KW - 1, 0, 2 * NA_KW - 2)
    dc_hot = jnp.asarray(np.eye(2 * NA_KW - 1, dtype=np.float32)[dc_idx])
    blocks = jnp.einsum("lhab,qcb->lhaqc", rpb, dc_hot, precision=lax.Precision.HIGHEST) * LOG2E
    blocks = jnp.where(jnp.asarray(col_ok)[None, None, None], blocks, -jnp.inf)
    ext = jnp.concatenate([blocks, jnp.full((depth, NA_HEADS, 1, GRID_W, GRID_W), -jnp.inf, F32)], axis=2)
    index = []
    for g in (0, 1, n_groups - 1):
        r = NA_QROWS * g + np.arange(NA_QROWS)
        base = int(np.clip(NA_QROWS * g - NA_KH // 2, 0, rows - NA_KROWS))
        krow = base + np.arange(NA_KROWS)
        rs = np.clip(r - NA_KH // 2, 0, rows - NA_KH)
        row_ok = (krow[None, :] >= rs[:, None]) & (krow[None, :] < rs[:, None] + NA_KH)
        dr_idx = krow[None, :] - r[:, None] + NA_KH - 1
        index.append(tuple(tuple(int(dr_idx[q, k]) if row_ok[q, k] else n_off for k in range(NA_KROWS))
                           for q in range(NA_QROWS)))
    tq, tk = NA_QROWS * GRID_W, NA_KROWS * GRID_W
    return pl.pallas_call(
        functools.partial(_bias_assemble_kernel, index=tuple(index)),
        grid=(depth, NA_HEADS),
        in_specs=[pl.BlockSpec((1, 1, n_off + 1, GRID_W, GRID_W), lambda l, h: (l, h, 0, 0, 0))],
        out_specs=pl.BlockSpec((1, len(index), 1, tq, tk), lambda l, h: (l, 0, h, 0, 0)),
        out_shape=jax.ShapeDtypeStruct((depth, len(index), NA_HEADS, tq, tk), F32),
        compiler_params=pltpu.CompilerParams(dimension_semantics=("parallel", "parallel")),
        name="na_bias",
    )(ext)


def _na_kernel(q_ref, k_ref, v_ref, *rest, rows):
    bias_refs, o_ref = rest[:NA_GROUPS], rest[NA_GROUPS]
    tq = NA_QROWS * GRID_W
    for u in range(NA_GROUPS):
        g = pl.program_id(1) * NA_GROUPS + u
        _na_group(q_ref.at[:, u * tq:(u + 1) * tq], k_ref, v_ref, bias_refs[u], o_ref.at[:, u * tq:(u + 1) * tq],
                  g, rows)


def _na_group(q_ref, k_ref, v_ref, bias_ref, o_ref, g, rows):
    start = pl.multiple_of(_na_key_base(g, rows) * GRID_W, GRID_W)
    nk = NA_KROWS * GRID_W
    tq = NA_QROWS * GRID_W
    lo_half = lax.broadcasted_iota(jnp.int32, (1, LANES), 1) < NA_HEAD_DIM
    outs = []
    for pair in range(NA_HEADS // 2):
        cols = slice(pair * LANES, (pair + 1) * LANES)
        qp = q_ref[0, :, cols]
        kp = k_ref[0, pl.ds(start, nk), cols]
        vp = v_ref[0, pl.ds(start, nk), cols]
        q2 = jnp.concatenate([jnp.where(lo_half, qp, 0), jnp.where(lo_half, 0, qp)], axis=0)
        s = lax.dot_general(q2, kp, (((1,), (1,)), ((), ())), preferred_element_type=F32)
        s = s + bias_ref[0, 2 * pair:2 * pair + 2].reshape(2 * tq, nk)
        e = jnp.exp2(s - jnp.max(s, axis=-1, keepdims=True))
        inv = 1.0 / jnp.sum(e, axis=-1, keepdims=True)
        e = e.astype(BF16)
        e2 = jnp.concatenate([e[:tq], e[tq:]], axis=1)
        v2 = jnp.concatenate([jnp.where(lo_half, vp, 0), jnp.where(lo_half, 0, vp)], axis=0)
        outs.append(_dot(e2, v2) * jnp.where(lo_half, inv[:tq], inv[tq:]))
    o_ref[0] = _rms(jnp.concatenate(outs, axis=-1))


def _na(qkv, bias, layer, batch, seq):
    rows = seq // GRID_W
    n_groups = rows // NA_QROWS
    tq = NA_QROWS * GRID_W
    tk = NA_KROWS * GRID_W

    def pattern(u):
        def index(b, s):
            g = s * NA_GROUPS + u
            return (layer, jnp.where(g == 0, 0, jnp.where(g == n_groups - 1, 2, 1)), 0, 0, 0)
        return index

    return pl.pallas_call(
        functools.partial(_na_kernel, rows=rows),
        grid=(batch, n_groups // NA_GROUPS),
        in_specs=[pl.BlockSpec((1, NA_GROUPS * tq, NA_WIDTH), lambda b, s: (b, s, 0)),
                  pl.BlockSpec((1, seq, NA_WIDTH), lambda b, s: (b, 0, 1)),
                  pl.BlockSpec((1, seq, NA_WIDTH), lambda b, s: (b, 0, 2))]
                 + [pl.BlockSpec((None, 1, NA_HEADS, tq, tk), pattern(u)) for u in range(NA_GROUPS)],
        out_specs=pl.BlockSpec((1, NA_GROUPS * tq, NA_WIDTH), lambda b, s: (b, s, 0)),
        out_shape=jax.ShapeDtypeStruct((batch, seq, NA_WIDTH), F32),
        compiler_params=pltpu.CompilerParams(dimension_semantics=("parallel", "arbitrary"),
                                             vmem_limit_bytes=VMEM_LIMIT),
        name="na",
    )(qkv, qkv, qkv, *([bias] * NA_GROUPS))


def _dft_mats(seq):
    half = seq // 2
    m = jnp.arange(half, dtype=jnp.int32)[None, :]
    a = jnp.arange(half // DFT_SPLIT, dtype=jnp.int32)[:, None]
    b = jnp.arange(DFT_SPLIT, dtype=jnp.int32)[:, None]
    ang_a = ((DFT_SPLIT * a * m) % seq).astype(F32) * (2.0 * math.pi / seq)
    ang_b = ((b * m) % seq).astype(F32) * (2.0 * math.pi / seq)
    ca, sa = jnp.cos(ang_a)[:, None, :], jnp.sin(ang_a)[:, None, :]
    cb, sb = jnp.cos(ang_b)[None, :, :], jnp.sin(ang_b)[None, :, :]
    cmat = (ca * cb - sa * sb).reshape(half, half)
    smat = (sa * cb + ca * sb).reshape(half, half)
    ang_t = jnp.arange(half, dtype=F32)[:, None] * (math.pi / seq)
    return cmat.astype(BF16), smat.astype(BF16), jnp.cos(ang_t), jnp.sin(ang_t)


def _filter_features(seq):
    bands = (FILTER_EMB - 1) // 2
    t = jnp.linspace(0.0, 1.0, seq, dtype=F32)[:, None]
    w = 2.0 * math.pi * jnp.arange(seq, dtype=F32)[:, None] / seq
    f = jnp.linspace(1e-4, bands - 1, bands, dtype=F32)[None, :]
    z = jnp.concatenate([t, jnp.cos(f * w), -jnp.sin(f * w)], axis=-1)
    return t, jnp.pad(z, ((0, 0), (0, LANES - FILTER_EMB)))


def _half_spectra(xe, xo, c_ref, s_ref, ct, st):
    nc = xe.shape[1]
    x = jnp.concatenate([xe, xo], axis=-1)
    rc = _dot(c_ref[...], x)
    rs = _dot(s_ref[...], x)
    ae, ao = rc[:, :nc], rc[:, nc:]
    be, bo = rs[:, :nc], rs[:, nc:]
    tr = ct * ao - st * bo
    ti = ct * bo + st * ao
    return ae + tr, -(be + ti), ae - tr, ti - be


def _alternating(n):
    row = lax.broadcasted_iota(jnp.int32, (n, 1), 0)
    return jnp.where((row & 1) == 0, 1.0, -1.0)


def _filt_kernel(z_ref, w1_ref, b1_ref, w2_ref, b2_ref, w3_ref, b3_ref, w4_ref, fr_ref, t_ref,
                 dl_ref, c_ref, s_ref, ct_ref, st_ref, kpr_ref, kpi_ref, kqr_ref, kqi_ref, kmid_ref):
    half = z_ref.shape[0] // 2
    n_fft = 4 * half
    hp = lax.Precision.HIGHEST
    fr = fr_ref[...]
    h = jnp.sin(fr * (jnp.dot(z_ref[...], w1_ref[...], precision=hp, preferred_element_type=F32) + b1_ref[...]))
    h = jnp.sin(fr * (jnp.dot(h, w2_ref[...], precision=hp, preferred_element_type=F32) + b2_ref[...]))
    h = jnp.sin(fr * (jnp.dot(h, w3_ref[...], precision=hp, preferred_element_type=F32) + b3_ref[...]))
    h = jnp.dot(h, w4_ref[...], precision=hp, preferred_element_type=F32)
    decay = jnp.exp(-t_ref[...] * dl_ref[...])
    h = h * jnp.concatenate([decay, decay], axis=-1)
    he, ho = h[:half], h[half:]
    first = lax.broadcasted_iota(jnp.int32, (half, 1), 0) == 0
    hf0, hb0 = he[:, :HY_WIDTH], he[:, HY_WIDTH:]
    tot = jnp.sum(jnp.abs(he) + jnp.abs(ho), axis=0, keepdims=True)
    lag0 = jnp.sum(jnp.where(first, jnp.abs(hf0 + hb0) - jnp.abs(hf0) - jnp.abs(hb0), 0.0), axis=0, keepdims=True)
    inv = 1.0 / (tot[:, :HY_WIDTH] + tot[:, HY_WIDTH:] + lag0)

    ct, st = ct_ref[...], st_ref[...]
    fpr, fpi, fqr, fqi = _half_spectra(he[:, :HY_WIDTH].astype(BF16), ho[:, :HY_WIDTH].astype(BF16),
                                       c_ref, s_ref, ct, st)
    bpr, bpi, bqr, bqi = _half_spectra(he[:, HY_WIDTH:].astype(BF16), ho[:, HY_WIDTH:].astype(BF16),
                                       c_ref, s_ref, ct, st)
    scale = inv * jnp.where(first, 1.0 / n_fft, 2.0 / n_fft)
    kpr_ref[...] = (fpr + bpr) * scale
    kpi_ref[...] = (fpi - bpi) * scale
    kqr_ref[...] = (fqr + bqr) * scale
    kqi_ref[...] = (fqi - bqi) * scale
    alt = _alternating(half)
    a1 = jnp.sum(he * alt, axis=0, keepdims=True)
    b1 = jnp.sum(ho * alt, axis=0, keepdims=True)
    mid = inv * (2.0 / n_fft)
    kmid_ref[0:1, :] = (a1[:, :HY_WIDTH] + a1[:, HY_WIDTH:]) * mid
    kmid_ref[1:2, :] = (b1[:, HY_WIDTH:] - b1[:, :HY_WIDTH]) * mid


def _filter_spectrum(z, t, deltas, cmat, smat, ct, st, w1, b1, w2, b2, w3, b3, w4, freq):
    half = z.shape[0] // 2
    pad2 = lambda w, r, c: jnp.pad(w, ((0, r - w.shape[0]), (0, c - w.shape[1])))
    even_then_odd = lambda a: jnp.concatenate([a[0::2], a[1::2]], axis=0)
    args = (even_then_odd(z), pad2(w1, LANES, LANES), pad2(b1[None], 1, LANES), pad2(w2, LANES, LANES),
            pad2(b2[None], 1, LANES), pad2(w3, LANES, LANES), pad2(b3[None], 1, LANES),
            pad2(w4, LANES, 2 * HY_WIDTH), pad2(freq[None], 1, LANES), even_then_odd(t), deltas, cmat, smat, ct, st)
    spec = jax.ShapeDtypeStruct((half, HY_WIDTH), F32)
    return pl.pallas_call(
        _filt_kernel,
        out_shape=[spec, spec, spec, spec, jax.ShapeDtypeStruct((2, HY_WIDTH), F32)],
        compiler_params=pltpu.CompilerParams(vmem_limit_bytes=VMEM_LIMIT),
        name="filt",
    )(*args)


def _short_conv(u_ref, w_ref, b_ref):
    half = u_ref.shape[0] // 2
    row = lax.broadcasted_iota(jnp.int32, (half, 1), 0)
    ue = u_ref[pl.ds(0, half, stride=2), :]
    uo = u_ref[pl.ds(1, half, stride=2), :]
    uo_prev = jnp.where(row == 0, 0.0, pltpu.roll(uo, 1, 0))
    ue_next = jnp.where(row == half - 1, 0.0, pltpu.roll(ue, half - 1, 0))
    w0, w1, w2, b = w_ref[0:1, :], w_ref[1:2, :], w_ref[2:3, :], b_ref[...]
    return uo_prev * w0 + ue * w1 + uo * w2 + b, ue * w0 + uo * w1 + ue_next * w2 + b


def _hyena_tile(x0_ref, x1_ref, hv_ref, w0_ref, w1_ref, wv_ref, b0_ref, b1_ref, bv_ref,
                kpr_ref, kpi_ref, kqr_ref, kqi_ref, kmid_ref, hb_ref, c_ref, s_ref, ct_ref, st_ref, o_ref):
    half = x0_ref.shape[0] // 2
    nc = x0_ref.shape[1]
    ct, st = ct_ref[...], st_ref[...]
    hve, hvo = _short_conv(hv_ref, wv_ref, bv_ref)
    x1e, x1o = _short_conv(x1_ref, w1_ref, b1_ref)
    ve, vo = hve * x1e, hvo * x1o
    pr, pi, qr, qi = _half_spectra(ve.astype(BF16), vo.astype(BF16), c_ref, s_ref, ct, st)
    kpr, kpi, kqr, kqi = kpr_ref[...], kpi_ref[...], kqr_ref[...], kqi_ref[...]
    ypr, ypi = pr * kpr - pi * kpi, pr * kpi + pi * kpr
    yqr, yqi = qr * kqr - qi * kqi, qr * kqi + qi * kqr
    dr, di = ypr - yqr, ypi - yqi
    zr = jnp.concatenate([(ypr + yqr).astype(BF16), (dr * ct - di * st).astype(BF16)], axis=-1)
    zi = jnp.concatenate([(ypi + yqi).astype(BF16), (dr * st + di * ct).astype(BF16)], axis=-1)
    y = _dot(c_ref[...], zr) - _dot(s_ref[...], zi)
    alt = _alternating(half)
    a1 = jnp.sum(ve * alt, axis=0, keepdims=True)
    b1 = jnp.sum(vo * alt, axis=0, keepdims=True)
    kmr, kmi = kmid_ref[0:1, :], kmid_ref[1:2, :]
    ye = y[:, :nc] + alt * (a1 * kmr + b1 * kmi)
    yo = y[:, nc:] - alt * (a1 * kmi - b1 * kmr)
    x0e, x0o = _short_conv(x0_ref, w0_ref, b0_ref)
    hb = hb_ref[...]
    o_ref[pl.ds(0, half, stride=2), :] = x0e * (ye + ve * hb)
    o_ref[pl.ds(1, half, stride=2), :] = x0o * (yo + vo * hb)


HY_TILE_ARGS = 15


def _hyena_kernel(*refs):
    n = HY_SUB * HY_TILE_ARGS
    shared, o_ref, tile_out = refs[n:n + 4], refs[n + 4], refs[n + 5:]
    for t in range(HY_SUB):
        _hyena_tile(*refs[t * HY_TILE_ARGS:(t + 1) * HY_TILE_ARGS], *shared, tile_out[t])
        o_ref[:, t * HY_CTILE:(t + 1) * HY_CTILE] = tile_out[t][...]


def _hyena(hy, w_sc, b_sc, cmat, smat, ct, st, kspec, hy_bias, batch, seq):
    kpr, kpi, kqr, kqi, kmid = kspec
    nt = HY_WIDTH // HY_CTILE
    half = seq // 2
    in_specs, args = [], []
    for t in range(HY_SUB):
        def tile_spec(shape, off, mode=None, t=t):
            lead = (lambda b: (b, 0)) if len(shape) == 3 else (lambda b: (0,))
            return pl.BlockSpec(shape, lambda j, b: lead(b) + (off * nt + j * HY_SUB + t,), pipeline_mode=mode)

        data = lambda off: tile_spec((None, seq, HY_CTILE), off)
        rows_of = lambda r, off=0: tile_spec((r, HY_CTILE), off)
        spec = tile_spec((half, HY_CTILE), 0, pl.Buffered(1))
        in_specs += [data(0), data(1), data(2), rows_of(3, 0), rows_of(3, 1), rows_of(3, 2),
                     rows_of(1, 0), rows_of(1, 1), rows_of(1, 2), spec, spec, spec, spec, rows_of(2), rows_of(1)]
        args += [hy, hy, hy, w_sc, w_sc, w_sc, b_sc, b_sc, b_sc, kpr, kpi, kqr, kqi, kmid, hy_bias]
    in_specs += [_const_spec(cmat.shape), _const_spec(smat.shape), _const_spec(ct.shape), _const_spec(st.shape)]
    args += [cmat, smat, ct, st]
    return pl.pallas_call(
        _hyena_kernel,
        grid=(nt // HY_SUB, batch),
        in_specs=in_specs,
        out_specs=pl.BlockSpec((None, seq, HY_SUB * HY_CTILE), lambda j, b: (b, 0, j)),
        out_shape=jax.ShapeDtypeStruct((batch, seq, HY_WIDTH), F32),
        scratch_shapes=[pltpu.VMEM((seq, HY_CTILE), F32) for _ in range(HY_SUB)],
        compiler_params=pltpu.CompilerParams(dimension_semantics=("arbitrary", "arbitrary"),
                                             vmem_limit_bytes=VMEM_LIMIT),
        name="hyena",
    )(*args)


def kernel(x, p, g_ffa, w_ffa_gate, w_ffa_up, w_ffa_down, g_mix, w_in, na_rpb, w_sc, b_sc, w_f1, b_f1, w_f2, b_f2, w_f3, b_f3, w_f4, filt_freq, hy_bias, g_out, w_out, g_ffb, w_ffb_gate, w_ffb_up, w_ffb_down, g_ple, w_ple_gate, w_ple_proj, g_final):
    batch, seq, _ = x.shape
    depth = p.shape[0]
    m = batch * seq
    rows = seq // GRID_W
    bf = lambda w: w.astype(BF16)
    vec = lambda g: g[:, None, :]

    cmat, smat, ct, st = _dft_mats(seq)
    t, z = _filter_features(seq)
    deltas = jnp.abs(jnp.linspace(MIN_DECAY, MAX_DECAY, HY_WIDTH, dtype=F32))[None, :]

    in_params = (vec(g_ffa), bf(w_ffa_gate), bf(w_ffa_up), bf(w_ffa_down), vec(g_mix), bf(w_in))
    out_params = (vec(g_out), bf(w_out), vec(g_ffb), bf(w_ffb_gate), bf(w_ffb_up), bf(w_ffb_down),
                  vec(g_ple), bf(w_ple_gate), bf(w_ple_proj))
    pr = p.reshape(depth, m, PLE_DIM)

    bias = _na_bias_tables(na_rpb, rows)

    xr = x.reshape(m, D_MODEL)
    for i in range(depth):
        xr, qkv, hy = _rows_in(xr, i, *in_params)
        yna = _na(qkv.reshape(batch, seq, 3 * NA_WIDTH), bias, i, batch, seq)
        kspec = _filter_spectrum(z, t, deltas, cmat, smat, ct, st, w_f1[i], b_f1[i], w_f2[i], b_f2[i],
                                 w_f3[i], b_f3[i], w_f4[i], filt_freq[i])
        yhy = _hyena(hy.reshape(batch, seq, 3 * HY_WIDTH), w_sc[i], b_sc[i][None], cmat, smat, ct, st, kspec,
                     hy_bias[i][None], batch, seq)
        xr = _rows_out(xr, yna.reshape(m, NA_WIDTH), yhy.reshape(m, HY_WIDTH), pr, i, *out_params,
                       g_final[None], final=(i == depth - 1))
    return xr.reshape(batch, seq, D_MODEL)
```

```python
import functools
import math

import numpy as np
import jax
import jax.numpy as jnp
from jax import lax
from jax.experimental import pallas as pl
from jax.experimental.pallas import tpu as pltpu

F32 = jnp.float32
BF16 = jnp.bfloat16

D_MODEL = 1024
GRID_W = 64
NA_HEADS = 8
NA_HEAD_DIM = 64
NA_WIDTH = NA_HEADS * NA_HEAD_DIM
NA_KH = 8
NA_KW = 16
HY_WIDTH = D_MODEL - NA_WIDTH
FILTER_EMB = 33
DECAY_TARGET = 1e-2
FAST_DECAY_PCT = 0.3
SLOW_DECAY_PCT = 1.5
MAX_DECAY = math.log(DECAY_TARGET) / FAST_DECAY_PCT
MIN_DECAY = math.log(DECAY_TARGET) / SLOW_DECAY_PCT
D_FF = 2816
PLE_DIM = 256
EPS = 1e-6

LANES = 128
ROW_TILE = 512
LOG2E = math.log2(math.e)
NA_QSCALE = NA_HEAD_DIM ** -0.5 * LOG2E
FF_CHUNKS = ((0, 1024), (1024, 2048), (2048, D_FF))
NA_QROWS = 4
NA_GROUPS = 2
NA_KROWS = 12
HY_CTILE = 128
HY_SUB = 2
DFT_SPLIT = 64
VMEM_LIMIT = 56 * 1024 * 1024


def _rms(x):
    return x * lax.rsqrt(jnp.mean(x * x, axis=-1, keepdims=True) + EPS)


def _dot(a, b):
    return jnp.dot(a, b, preferred_element_type=F32)


def _swiglu_residual(x, g, wg_ref, wu_ref, wd_ref):
    h = (_rms(x) * g).astype(BF16)
    acc = None
    for lo, hi in FF_CHUNKS:
        gate = _dot(h, wg_ref[:, lo:hi])
        up = _dot(h, wu_ref[:, lo:hi])
        act = (gate * jax.nn.sigmoid(gate) * up).astype(BF16)
        part = _dot(act, wd_ref[lo:hi, :])
        acc = part if acc is None else acc + part
    return x + 0.5 * acc


def _const_spec(shape):
    nd = len(shape)
    return pl.BlockSpec(shape, lambda *_: (0,) * nd, pipeline_mode=pl.Buffered(1))


def _layer_spec(stacked, layer):
    return pl.BlockSpec((None,) + stacked.shape[1:], lambda *_: (layer, 0, 0), pipeline_mode=pl.Buffered(1))


def _row_spec(width):
    return pl.BlockSpec((ROW_TILE, width), lambda i: (i, 0))


def _rows_in_kernel(x_ref, g_ffa_ref, wg_ref, wu_ref, wd_ref, g_mix_ref, w_in_ref,
                    x_out_ref, qkv_ref, hy_ref):
    x = _swiglu_residual(x_ref[...], g_ffa_ref[...], wg_ref, wu_ref, wd_ref)
    x_out_ref[...] = x
    h = (_rms(x) * g_mix_ref[...]).astype(BF16)
    nq = 3 * NA_WIDTH
    qkv = _dot(h, w_in_ref[:, :nq])
    qkv_ref[:, :NA_WIDTH] = (qkv[:, :NA_WIDTH] * NA_QSCALE).astype(BF16)
    qkv_ref[:, NA_WIDTH:] = qkv[:, NA_WIDTH:].astype(BF16)
    hy_ref[...] = _dot(h, w_in_ref[:, nq:])


def _rows_in(x, layer, g_ffa, wg, wu, wd, g_mix, w_in):
    m = x.shape[0]
    return pl.pallas_call(
        _rows_in_kernel,
        grid=(m // ROW_TILE,),
        in_specs=[_row_spec(D_MODEL)] + [_layer_spec(c, layer) for c in (g_ffa, wg, wu, wd, g_mix, w_in)],
        out_specs=[_row_spec(D_MODEL), _row_spec(3 * NA_WIDTH), _row_spec(3 * HY_WIDTH)],
        out_shape=[jax.ShapeDtypeStruct((m, D_MODEL), F32),
                   jax.ShapeDtypeStruct((m, 3 * NA_WIDTH), BF16),
                   jax.ShapeDtypeStruct((m, 3 * HY_WIDTH), F32)],
        compiler_params=pltpu.CompilerParams(dimension_semantics=("parallel",), vmem_limit_bytes=VMEM_LIMIT),
        name="rows_in",
    )(x, g_ffa, wg, wu, wd, g_mix, w_in)


def _rows_out_kernel(x_ref, yna_ref, yhy_ref, p_ref, g_out_ref, w_out_ref, g_ffb_ref, wg_ref, wu_ref, wd_ref,
                     g_ple_ref, w_pg_ref, w_pp_ref, g_fin_ref, o_ref, *, final):
    g_out = g_out_ref[...]
    y = jnp.concatenate([(yna_ref[...] * g_out[:, :NA_WIDTH]).astype(BF16),
                         (_rms(yhy_ref[...]) * g_out[:, NA_WIDTH:]).astype(BF16)], axis=-1)
    x = x_ref[...] + _dot(y, w_out_ref[...])
    x = _swiglu_residual(x, g_ffb_ref[...], wg_ref, wu_ref, wd_ref)
    h = (_rms(x) * g_ple_ref[...]).astype(BF16)
    gate = jax.nn.sigmoid(_dot(h, w_pg_ref[...]))
    x = x + gate * _dot(p_ref[...].astype(BF16), w_pp_ref[...])
    if final:
        x = _rms(x) * g_fin_ref[...]
    o_ref[...] = x


def _rows_out(x, yna, yhy, p, layer, g_out, w_out, g_ffb, wg, wu, wd, g_ple, w_pg, w_pp, g_fin, final):
    m = x.shape[0]
    stacks = (g_out, w_out, g_ffb, wg, wu, wd, g_ple, w_pg, w_pp)
    return pl.pallas_call(
        functools.partial(_rows_out_kernel, final=final),
        grid=(m // ROW_TILE,),
        in_specs=[_row_spec(D_MODEL), _row_spec(NA_WIDTH), _row_spec(HY_WIDTH),
                  pl.BlockSpec((None, ROW_TILE, PLE_DIM), lambda i: (layer, i, 0))]
                 + [_layer_spec(c, layer) for c in stacks] + [_const_spec(g_fin.shape)],
        out_specs=_row_spec(D_MODEL),
        out_shape=jax.ShapeDtypeStruct((m, D_MODEL), F32),
        compiler_params=pltpu.CompilerParams(dimension_semantics=("parallel",), vmem_limit_bytes=VMEM_LIMIT),
        name="rows_out",
    )(x, yna, yhy, p, *stacks, g_fin)


def _na_key_base(g, rows):
    return jnp.clip(NA_QROWS * g - NA_KH // 2, 0, rows - NA_KROWS)


def _bias_assemble_kernel(ext_ref, o_ref, *, index):
    for p, per_q in enumerate(index):
        for q, per_k in enumerate(per_q):
            for k, block in enumerate(per_k):
                o_ref[0, p, 0, q * GRID_W:(q + 1) * GRID_W, k * GRID_W:(k + 1) * GRID_W] = ext_ref[0, 0, block]


def _na_bias_tables(rpb, rows):
    depth = rpb.shape[0]
    n_groups = rows // NA_QROWS
    n_off = 2 * NA_KH - 1
    qc = np.arange(GRID_W)
    cs = np.clip(qc - NA_KW // 2, 0, GRID_W - NA_KW)
    kc = np.arange(GRID_W)
    col_ok = (kc[None, :] >= cs[:, None]) & (kc[None, :] < cs[:, None] + NA_KW)
    dc_idx = np.clip(kc[None, :] - qc[:, None] + NA_KW - 1, 0, 2 * NA_KW - 2)
    dc_hot = jnp.asarray(np.eye(2 * NA_KW - 1, dtype=np.float32)[dc_idx])
    blocks = jnp.einsum("lhab,qcb->lhaqc", rpb, dc_hot, precision=lax.Precision.HIGHEST) * LOG2E
    blocks = jnp.where(jnp.asarray(col_ok)[None, None, None], blocks, -jnp.inf)
    ext = jnp.concatenate([blocks, jnp.full((depth, NA_HEADS, 1, GRID_W, GRID_W), -jnp.inf, F32)], axis=2)
    index = []
    for g in (0, 1, n_groups - 1):
        r = NA_QROWS * g + np.arange(NA_QROWS)
        base = int(np.clip(NA_QROWS * g - NA_KH // 2, 0, rows - NA_KROWS))
        krow = base + np.arange(NA_KROWS)
        rs = np.clip(r - NA_KH // 2, 0, rows - NA_KH)
        row_ok = (krow[None, :] >= rs[:, None]) & (krow[None, :] < rs[:, None] + NA_KH)
        dr_idx = krow[None, :] - r[:, None] + NA_KH - 1
        index.append(tuple(tuple(int(dr_idx[q, k]) if row_ok[q, k] else n_off for k in range(NA_KROWS))
                           for q in range(NA_QROWS)))
    tq, tk = NA_QROWS * GRID_W, NA_KROWS * GRID_W
    return pl.pallas_call(
        functools.partial(_bias_assemble_kernel, index=tuple(index)),
        grid=(depth, NA_HEADS),
        in_specs=[pl.BlockSpec((1, 1, n_off + 1, GRID_W, GRID_W), lambda l, h: (l, h, 0, 0, 0))],
        out_specs=pl.BlockSpec((1, len(index), 1, tq, tk), lambda l, h: (l, 0, h, 0, 0)),
        out_shape=jax.ShapeDtypeStruct((depth, len(index), NA_HEADS, tq, tk), F32),
        compiler_params=pltpu.CompilerParams(dimension_semantics=("parallel", "parallel")),
        name="na_bias",
    )(ext)


def _na_kernel(q_ref, k_ref, v_ref, *rest, rows):
    bias_refs, o_ref = rest[:NA_GROUPS], rest[NA_GROUPS]
    tq = NA_QROWS * GRID_W
    for u in range(NA_GROUPS):
        g = pl.program_id(1) * NA_GROUPS + u
        _na_group(q_ref.at[:, u * tq:(u + 1) * tq], k_ref, v_ref, bias_refs[u], o_ref.at[:, u * tq:(u + 1) * tq],
                  g, rows)


def _na_group(q_ref, k_ref, v_ref, bias_ref, o_ref, g, rows):
    start = pl.multiple_of(_na_key_base(g, rows) * GRID_W, GRID_W)
    nk = NA_KROWS * GRID_W
    tq = NA_QROWS * GRID_W
    lo_half = lax.broadcasted_iota(jnp.int32, (1, LANES), 1) < NA_HEAD_DIM
    outs = []
    for pair in range(NA_HEADS // 2):
        cols = slice(pair * LANES, (pair + 1) * LANES)
        qp = q_ref[0, :, cols]
        kp = k_ref[0, pl.ds(start, nk), cols]
        vp = v_ref[0, pl.ds(start, nk), cols]
        q2 = jnp.concatenate([jnp.where(lo_half, qp, 0), jnp.where(lo_half, 0, qp)], axis=0)
        s = lax.dot_general(q2, kp, (((1,), (1,)), ((), ())), preferred_element_type=F32)
        s = s + bias_ref[0, 2 * pair:2 * pair + 2].reshape(2 * tq, nk)
        e = jnp.exp2(s - jnp.max(s, axis=-1, keepdims=True))
        inv = 1.0 / jnp.sum(e, axis=-1, keepdims=True)
        e = e.astype(BF16)
        e2 = jnp.concatenate([e[:tq], e[tq:]], axis=1)
        v2 = jnp.concatenate([jnp.where(lo_half, vp, 0), jnp.where(lo_half, 0, vp)], axis=0)
        outs.append(_dot(e2, v2) * jnp.where(lo_half, inv[:tq], inv[tq:]))
    o_ref[0] = _rms(jnp.concatenate(outs, axis=-1))


def _na(qkv, bias, layer, batch, seq):
    rows = seq // GRID_W
    n_groups = rows // NA_QROWS
    tq = NA_QROWS * GRID_W
    tk = NA_KROWS * GRID_W

    def pattern(u):
        def index(b, s):
            g = s * NA_GROUPS + u
            return (layer, jnp.where(g == 0, 0, jnp.where(g == n_groups - 1, 2, 1)), 0, 0, 0)
        return index

    return pl.pallas_call(
        functools.partial(_na_kernel, rows=rows),
        grid=(batch, n_groups // NA_GROUPS),
        in_specs=[pl.BlockSpec((1, NA_GROUPS * tq, NA_WIDTH), lambda b, s: (b, s, 0)),
                  pl.BlockSpec((1, seq, NA_WIDTH), lambda b, s: (b, 0, 1)),
                  pl.BlockSpec((1, seq, NA_WIDTH), lambda b, s: (b, 0, 2))]
                 + [pl.BlockSpec((None, 1, NA_HEADS, tq, tk), pattern(u)) for u in range(NA_GROUPS)],
        out_specs=pl.BlockSpec((1, NA_GROUPS * tq, NA_WIDTH), lambda b, s: (b, s, 0)),
        out_shape=jax.ShapeDtypeStruct((batch, seq, NA_WIDTH), F32),
        compiler_params=pltpu.CompilerParams(dimension_semantics=("parallel", "arbitrary"),
                                             vmem_limit_bytes=VMEM_LIMIT),
        name="na",
    )(qkv, qkv, qkv, *([bias] * NA_GROUPS))


def _dft_mats(seq):
    half = seq // 2
    m = jnp.arange(half, dtype=jnp.int32)[None, :]
    a = jnp.arange(half // DFT_SPLIT, dtype=jnp.int32)[:, None]
    b = jnp.arange(DFT_SPLIT, dtype=jnp.int32)[:, None]
    ang_a = ((DFT_SPLIT * a * m) % seq).astype(F32) * (2.0 * math.pi / seq)
    ang_b = ((b * m) % seq).astype(F32) * (2.0 * math.pi / seq)
    ca, sa = jnp.cos(ang_a)[:, None, :], jnp.sin(ang_a)[:, None, :]
    cb, sb = jnp.cos(ang_b)[None, :, :], jnp.sin(ang_b)[None, :, :]
    cmat = (ca * cb - sa * sb).reshape(half, half)
    smat = (sa * cb + ca * sb).reshape(half, half)
    ang_t = jnp.arange(half, dtype=F32)[:, None] * (math.pi / seq)
    return cmat.astype(BF16), smat.astype(BF16), jnp.cos(ang_t), jnp.sin(ang_t)


def _filter_features(seq):
    bands = (FILTER_EMB - 1) // 2
    t = jnp.linspace(0.0, 1.0, seq, dtype=F32)[:, None]
    w = 2.0 * math.pi * jnp.arange(seq, dtype=F32)[:, None] / seq
    f = jnp.linspace(1e-4, bands - 1, bands, dtype=F32)[None, :]
    z = jnp.concatenate([t, jnp.cos(f * w), -jnp.sin(f * w)], axis=-1)
    return t, jnp.pad(z, ((0, 0), (0, LANES - FILTER_EMB)))


def _half_spectra(xe, xo, c_ref, s_ref, ct, st):
    nc = xe.shape[1]
    x = jnp.concatenate([xe, xo], axis=-1)
    rc = _dot(c_ref[...], x)
    rs = _dot(s_ref[...], x)
    ae, ao = rc[:, :nc], rc[:, nc:]
    be, bo = rs[:, :nc], rs[:, nc:]
    tr = ct * ao - st * bo
    ti = ct * bo + st * ao
    return ae + tr, -(be + ti), ae - tr, ti - be


def _alternating(n):
    row = lax.broadcasted_iota(jnp.int32, (n, 1), 0)
    return jnp.where((row & 1) == 0, 1.0, -1.0)


def _filt_kernel(z_ref, w1_ref, b1_ref, w2_ref, b2_ref, w3_ref, b3_ref, w4_ref, fr_ref, t_ref,
                 dl_ref, c_ref, s_ref, ct_ref, st_ref, kpr_ref, kpi_ref, kqr_ref, kqi_ref, kmid_ref):
    half = z_ref.shape[0] // 2
    n_fft = 4 * half
    hp = lax.Precision.HIGHEST
    fr = fr_ref[...]
    h = jnp.sin(fr * (jnp.dot(z_ref[...], w1_ref[...], precision=hp, preferred_element_type=F32) + b1_ref[...]))
    h = jnp.sin(fr * (jnp.dot(h, w2_ref[...], precision=hp, preferred_element_type=F32) + b2_ref[...]))
    h = jnp.sin(fr * (jnp.dot(h, w3_ref[...], precision=hp, preferred_element_type=F32) + b3_ref[...]))
    h = jnp.dot(h, w4_ref[...], precision=hp, preferred_element_type=F32)
    decay = jnp.exp(-t_ref[...] * dl_ref[...])
    h = h * jnp.concatenate([decay, decay], axis=-1)
    he, ho = h[:half], h[half:]
    first = lax.broadcasted_iota(jnp.int32, (half, 1), 0) == 0
    hf0, hb0 = he[:, :HY_WIDTH], he[:, HY_WIDTH:]
    tot = jnp.sum(jnp.abs(he) + jnp.abs(ho), axis=0, keepdims=True)
    lag0 = jnp.sum(jnp.where(first, jnp.abs(hf0 + hb0) - jnp.abs(hf0) - jnp.abs(hb0), 0.0), axis=0, keepdims=True)
    inv = 1.0 / (tot[:, :HY_WIDTH] + tot[:, HY_WIDTH:] + lag0)

    ct, st = ct_ref[...], st_ref[...]
    fpr, fpi, fqr, fqi = _half_spectra(he[:, :HY_WIDTH].astype(BF16), ho[:, :HY_WIDTH].astype(BF16),
                                       c_ref, s_ref, ct, st)
    bpr, bpi, bqr, bqi = _half_spectra(he[:, HY_WIDTH:].astype(BF16), ho[:, HY_WIDTH:].astype(BF16),
                                       c_ref, s_ref, ct, st)
    scale = inv * jnp.where(first, 1.0 / n_fft, 2.0 / n_fft)
    kpr_ref[...] = (fpr + bpr) * scale
    kpi_ref[...] = (fpi - bpi) * scale
    kqr_ref[...] = (fqr + bqr) * scale
    kqi_ref[...] = (fqi - bqi) * scale
    alt = _alternating(half)
    a1 = jnp.sum(he * alt, axis=0, keepdims=True)
    b1 = jnp.sum(ho * alt, axis=0, keepdims=True)
    mid = inv * (2.0 / n_fft)
    kmid_ref[0:1, :] = (a1[:, :HY_WIDTH] + a1[:, HY_WIDTH:]) * mid
    kmid_ref[1:2, :] = (b1[:, HY_WIDTH:] - b1[:, :HY_WIDTH]) * mid


def _filter_spectrum(z, t, deltas, cmat, smat, ct, st, w1, b1, w2, b2, w3, b3, w4, freq):
    depth = w1.shape[0]
    half = z.shape[0] // 2
    pad = lambda w, r, c: jnp.pad(w, ((0, 0), (0, r - w.shape[1]), (0, c - w.shape[2])))
    row = lambda v: v[:, None, :]
    even_then_odd = lambda a: jnp.concatenate([a[0::2], a[1::2]], axis=0)
    params = (pad(w1, LANES, LANES), pad(row(b1), 1, LANES), pad(w2, LANES, LANES), pad(row(b2), 1, LANES),
              pad(w3, LANES, LANES), pad(row(b3), 1, LANES), pad(w4, LANES, 2 * HY_WIDTH), pad(row(freq), 1, LANES))
    consts = (even_then_odd(t), deltas, cmat, smat, ct, st)
    per_layer = lambda a: pl.BlockSpec((None,) + a.shape[1:], lambda l: (l, 0, 0))
    spec = jax.ShapeDtypeStruct((depth, half, HY_WIDTH), F32)
    mid = jax.ShapeDtypeStruct((depth, 2, HY_WIDTH), F32)
    return pl.pallas_call(
        _filt_kernel,
        grid=(depth,),
        in_specs=[_const_spec((2 * half, LANES))] + [per_layer(a) for a in params]
                 + [_const_spec(c.shape) for c in consts],
        out_specs=[per_layer(spec), per_layer(spec), per_layer(spec), per_layer(spec), per_layer(mid)],
        out_shape=[spec, spec, spec, spec, mid],
        compiler_params=pltpu.CompilerParams(dimension_semantics=("parallel",), vmem_limit_bytes=VMEM_LIMIT),
        name="filt",
    )(even_then_odd(z), *params, *consts)


def _short_conv(u_ref, w_ref, b_ref):
    half = u_ref.shape[0] // 2
    row = lax.broadcasted_iota(jnp.int32, (half, 1), 0)
    ue = u_ref[pl.ds(0, half, stride=2), :]
    uo = u_ref[pl.ds(1, half, stride=2), :]
    uo_prev = jnp.where(row == 0, 0.0, pltpu.roll(uo, 1, 0))
    ue_next = jnp.where(row == half - 1, 0.0, pltpu.roll(ue, half - 1, 0))
    w0, w1, w2, b = w_ref[0:1, :], w_ref[1:2, :], w_ref[2:3, :], b_ref[...]
    return uo_prev * w0 + ue * w1 + uo * w2 + b, ue * w0 + uo * w1 + ue_next * w2 + b


def _hyena_tile(x0_ref, x1_ref, hv_ref, w0_ref, w1_ref, wv_ref, b0_ref, b1_ref, bv_ref,
                kpr_ref, kpi_ref, kqr_ref, kqi_ref, kmid_ref, hb_ref, c_ref, s_ref, ct_ref, st_ref, o_ref):
    half = x0_ref.shape[0] // 2
    nc = x0_ref.shape[1]
    ct, st = ct_ref[...], st_ref[...]
    hve, hvo = _short_conv(hv_ref, wv_ref, bv_ref)
    x1e, x1o = _short_conv(x1_ref, w1_ref, b1_ref)
    ve, vo = hve * x1e, hvo * x1o
    pr, pi, qr, qi = _half_spectra(ve.astype(BF16), vo.astype(BF16), c_ref, s_ref, ct, st)
    kpr, kpi, kqr, kqi = kpr_ref[...], kpi_ref[...], kqr_ref[...], kqi_ref[...]
    ypr, ypi = pr * kpr - pi * kpi, pr * kpi + pi * kpr
    yqr, yqi = qr * kqr - qi * kqi, qr * kqi + qi * kqr
    dr, di = ypr - yqr, ypi - yqi
    zr = jnp.concatenate([(ypr + yqr).astype(BF16), (dr * ct - di * st).astype(BF16)], axis=-1)
    zi = jnp.concatenate([(ypi + yqi).astype(BF16), (dr * st + di * ct).astype(BF16)], axis=-1)
    y = _dot(c_ref[...], zr) - _dot(s_ref[...], zi)
    alt = _alternating(half)
    a1 = jnp.sum(ve * alt, axis=0, keepdims=True)
    b1 = jnp.sum(vo * alt, axis=0, keepdims=True)
    kmr, kmi = kmid_ref[0:1, :], kmid_ref[1:2, :]
    ye = y[:, :nc] + alt * (a1 * kmr + b1 * kmi)
    yo = y[:, nc:] - alt * (a1 * kmi - b1 * kmr)
    x0e, x0o = _short_conv(x0_ref, w0_ref, b0_ref)
    hb = hb_ref[...]
    o_ref[pl.ds(0, half, stride=2), :] = x0e * (ye + ve * hb)
    o_ref[pl.ds(1, half, stride=2), :] = x0o * (yo + vo * hb)


HY_TILE_ARGS = 15


def _hyena_kernel(*refs):
    n = HY_SUB * HY_TILE_ARGS
    shared, o_ref, tile_out = refs[n:n + 4], refs[n + 4], refs[n + 5:]
    for t in range(HY_SUB):
        _hyena_tile(*refs[t * HY_TILE_ARGS:(t + 1) * HY_TILE_ARGS], *shared, tile_out[t])
        o_ref[:, t * HY_CTILE:(t + 1) * HY_CTILE] = tile_out[t][...]


def _hyena(hy, w_sc, b_sc, cmat, smat, ct, st, kspec, layer, hy_bias, batch, seq):
    kpr, kpi, kqr, kqi, kmid = kspec
    nt = HY_WIDTH // HY_CTILE
    half = seq // 2
    in_specs, args = [], []
    for t in range(HY_SUB):
        def tile_spec(shape, off, mode=None, t=t):
            lead = (lambda b: (b, 0)) if len(shape) == 3 else (lambda b: (0,))
            return pl.BlockSpec(shape, lambda j, b: lead(b) + (off * nt + j * HY_SUB + t,), pipeline_mode=mode)

        data = lambda off: tile_spec((None, seq, HY_CTILE), off)
        rows_of = lambda r, off=0: tile_spec((r, HY_CTILE), off)
        spec = pl.BlockSpec((None, half, HY_CTILE), lambda j, b, t=t: (layer, 0, j * HY_SUB + t),
                            pipeline_mode=pl.Buffered(1))
        mid = pl.BlockSpec((None, 2, HY_CTILE), lambda j, b, t=t: (layer, 0, j * HY_SUB + t))
        in_specs += [data(0), data(1), data(2), rows_of(3, 0), rows_of(3, 1), rows_of(3, 2),
                     rows_of(1, 0), rows_of(1, 1), rows_of(1, 2), spec, spec, spec, spec, mid, rows_of(1)]
        args += [hy, hy, hy, w_sc, w_sc, w_sc, b_sc, b_sc, b_sc, kpr, kpi, kqr, kqi, kmid, hy_bias]
    in_specs += [_const_spec(cmat.shape), _const_spec(smat.shape), _const_spec(ct.shape), _const_spec(st.shape)]
    args += [cmat, smat, ct, st]
    return pl.pallas_call(
        _hyena_kernel,
        grid=(nt // HY_SUB, batch),
        in_specs=in_specs,
        out_specs=pl.BlockSpec((None, seq, HY_SUB * HY_CTILE), lambda j, b: (b, 0, j)),
        out_shape=jax.ShapeDtypeStruct((batch, seq, HY_WIDTH), F32),
        scratch_shapes=[pltpu.VMEM((seq, HY_CTILE), F32) for _ in range(HY_SUB)],
        compiler_params=pltpu.CompilerParams(dimension_semantics=("arbitrary", "arbitrary"),
                                             vmem_limit_bytes=VMEM_LIMIT),
        name="hyena",
    )(*args)


def kernel(x, p, g_ffa, w_ffa_gate, w_ffa_up, w_ffa_down, g_mix, w_in, na_rpb, w_sc, b_sc, w_f1, b_f1, w_f2, b_f2, w_f3, b_f3, w_f4, filt_freq, hy_bias, g_out, w_out, g_ffb, w_ffb_gate, w_ffb_up, w_ffb_down, g_ple, w_ple_gate, w_ple_proj, g_final):
    batch, seq, _ = x.shape
    depth = p.shape[0]
    m = batch * seq
    rows = seq // GRID_W
    bf = lambda w: w.astype(BF16)
    vec = lambda g: g[:, None, :]

    cmat, smat, ct, st = _dft_mats(seq)
    t, z = _filter_features(seq)
    deltas = jnp.abs(jnp.linspace(MIN_DECAY, MAX_DECAY, HY_WIDTH, dtype=F32))[None, :]

    in_params = (vec(g_ffa), bf(w_ffa_gate), bf(w_ffa_up), bf(w_ffa_down), vec(g_mix), bf(w_in))
    out_params = (vec(g_out), bf(w_out), vec(g_ffb), bf(w_ffb_gate), bf(w_ffb_up), bf(w_ffb_down),
                  vec(g_ple), bf(w_ple_gate), bf(w_ple_proj))
    pr = p.reshape(depth, m, PLE_DIM)

    bias = _na_bias_tables(na_rpb, rows)
    kspec = _filter_spectrum(z, t, deltas, cmat, smat, ct, st, w_f1, b_f1, w_f2, b_f2, w_f3, b_f3, w_f4, filt_freq)

    xr = x.reshape(m, D_MODEL)
    for i in range(depth):
        xr, qkv, hy = _rows_in(xr, i, *in_params)
        yna = _na(qkv.reshape(batch, seq, 3 * NA_WIDTH), bias, i, batch, seq)
        yhy = _hyena(hy.reshape(batch, seq, 3 * HY_WIDTH), w_sc[i], b_sc[i][None], cmat, smat, ct, st, kspec, i,
                     hy_bias[i][None], batch, seq)
        xr = _rows_out(xr, yna.reshape(m, NA_WIDTH), yhy.reshape(m, HY_WIDTH), pr, i, *out_params,
                       g_final[None], final=(i == depth - 1))
    return xr.reshape(batch, seq, D_MODEL)
```

```python
import functools
import math

import numpy as np
import jax
import jax.numpy as jnp
from jax import lax
from jax.experimental import pallas as pl
from jax.experimental.pallas import tpu as pltpu

F32 = jnp.float32
BF16 = jnp.bfloat16

D_MODEL = 1024
GRID_W = 64
NA_HEADS = 8
NA_HEAD_DIM = 64
NA_WIDTH = NA_HEADS * NA_HEAD_DIM
NA_KH = 8
NA_KW = 16
HY_WIDTH = D_MODEL - NA_WIDTH
FILTER_EMB = 33
DECAY_TARGET = 1e-2
FAST_DECAY_PCT = 0.3
SLOW_DECAY_PCT = 1.5
MAX_DECAY = math.log(DECAY_TARGET) / FAST_DECAY_PCT
MIN_DECAY = math.log(DECAY_TARGET) / SLOW_DECAY_PCT
D_FF = 2816
PLE_DIM = 256
EPS = 1e-6

LANES = 128
ROW_TILE = 512
LOG2E = math.log2(math.e)
NA_QSCALE = NA_HEAD_DIM ** -0.5 * LOG2E
FF_CHUNKS = ((0, 1024), (1024, 2048), (2048, D_FF))
NA_QROWS = 4
NA_GROUPS = 2
NA_KROWS = 12
HY_CTILE = 128
HY_SUB = 2
DFT_SPLIT = 64
VMEM_LIMIT = 56 * 1024 * 1024


def _rms(x):
    return x * lax.rsqrt(jnp.mean(x * x, axis=-1, keepdims=True) + EPS)


def _dot(a, b):
    return jnp.dot(a, b, preferred_element_type=F32)


def _swiglu_residual(x, g, wg_ref, wu_ref, wd_ref):
    h = (_rms(x) * g).astype(BF16)
    acc = None
    for lo, hi in FF_CHUNKS:
        gate = _dot(h, wg_ref[:, lo:hi])
        up = _dot(h, wu_ref[:, lo:hi])
        act = (gate * jax.nn.sigmoid(gate) * up).astype(BF16)
        part = _dot(act, wd_ref[lo:hi, :])
        acc = part if acc is None else acc + part
    return x + 0.5 * acc


def _const_spec(shape):
    nd = len(shape)
    return pl.BlockSpec(shape, lambda *_: (0,) * nd, pipeline_mode=pl.Buffered(1))


def _layer_spec(stacked, layer):
    return pl.BlockSpec((None,) + stacked.shape[1:], lambda *_: (layer, 0, 0), pipeline_mode=pl.Buffered(1))


def _row_spec(width):
    return pl.BlockSpec((ROW_TILE, width), lambda i: (i, 0))


def _rows_in_kernel(x_ref, g_ffa_ref, wg_ref, wu_ref, wd_ref, g_mix_ref, w_in_ref,
                    x_out_ref, qkv_ref, hy_ref):
    x = _swiglu_residual(x_ref[...], g_ffa_ref[...], wg_ref, wu_ref, wd_ref)
    x_out_ref[...] = x
    h = (_rms(x) * g_mix_ref[...]).astype(BF16)
    nq = 3 * NA_WIDTH
    qkv = _dot(h, w_in_ref[:, :nq])
    qkv_ref[:, :NA_WIDTH] = (qkv[:, :NA_WIDTH] * NA_QSCALE).astype(BF16)
    qkv_ref[:, NA_WIDTH:] = qkv[:, NA_WIDTH:].astype(BF16)
    hy_ref[...] = _dot(h, w_in_ref[:, nq:])


def _rows_in(x, layer, g_ffa, wg, wu, wd, g_mix, w_in):
    m = x.shape[0]
    return pl.pallas_call(
        _rows_in_kernel,
        grid=(m // ROW_TILE,),
        in_specs=[_row_spec(D_MODEL)] + [_layer_spec(c, layer) for c in (g_ffa, wg, wu, wd, g_mix, w_in)],
        out_specs=[_row_spec(D_MODEL), _row_spec(3 * NA_WIDTH), _row_spec(3 * HY_WIDTH)],
        out_shape=[jax.ShapeDtypeStruct((m, D_MODEL), F32),
                   jax.ShapeDtypeStruct((m, 3 * NA_WIDTH), BF16),
                   jax.ShapeDtypeStruct((m, 3 * HY_WIDTH), F32)],
        compiler_params=pltpu.CompilerParams(dimension_semantics=("parallel",), vmem_limit_bytes=VMEM_LIMIT),
        name="rows_in",
    )(x, g_ffa, wg, wu, wd, g_mix, w_in)


def _rows_out_kernel(x_ref, yna_ref, yhy_ref, p_ref, g_out_ref, w_out_ref, g_ffb_ref, wg_ref, wu_ref, wd_ref,
                     g_ple_ref, w_pg_ref, w_pp_ref, g_fin_ref, o_ref, *, final):
    g_out = g_out_ref[...]
    y = jnp.concatenate([(yna_ref[...] * g_out[:, :NA_WIDTH]).astype(BF16),
                         (_rms(yhy_ref[...]) * g_out[:, NA_WIDTH:]).astype(BF16)], axis=-1)
    x = x_ref[...] + _dot(y, w_out_ref[...])
    x = _swiglu_residual(x, g_ffb_ref[...], wg_ref, wu_ref, wd_ref)
    h = (_rms(x) * g_ple_ref[...]).astype(BF16)
    gate = jax.nn.sigmoid(_dot(h, w_pg_ref[...]))
    x = x + gate * _dot(p_ref[...].astype(BF16), w_pp_ref[...])
    if final:
        x = _rms(x) * g_fin_ref[...]
    o_ref[...] = x


def _rows_out(x, yna, yhy, p, layer, g_out, w_out, g_ffb, wg, wu, wd, g_ple, w_pg, w_pp, g_fin, final):
    m = x.shape[0]
    stacks = (g_out, w_out, g_ffb, wg, wu, wd, g_ple, w_pg, w_pp)
    return pl.pallas_call(
        functools.partial(_rows_out_kernel, final=final),
        grid=(m // ROW_TILE,),
        in_specs=[_row_spec(D_MODEL), _row_spec(NA_WIDTH), _row_spec(HY_WIDTH),
                  pl.BlockSpec((None, ROW_TILE, PLE_DIM), lambda i: (layer, i, 0))]
                 + [_layer_spec(c, layer) for c in stacks] + [_const_spec(g_fin.shape)],
        out_specs=_row_spec(D_MODEL),
        out_shape=jax.ShapeDtypeStruct((m, D_MODEL), F32),
        compiler_params=pltpu.CompilerParams(dimension_semantics=("parallel",), vmem_limit_bytes=VMEM_LIMIT),
        name="rows_out",
    )(x, yna, yhy, p, *stacks, g_fin)


def _na_key_base(g, rows):
    return jnp.clip(NA_QROWS * g - NA_KH // 2, 0, rows - NA_KROWS)


def _bias_assemble_kernel(ext_ref, o_ref, *, index):
    for p, per_q in enumerate(index):
        for q, per_k in enumerate(per_q):
            for k, block in enumerate(per_k):
                o_ref[0, p, 0, q * GRID_W:(q + 1) * GRID_W, k * GRID_W:(k + 1) * GRID_W] = ext_ref[0, 0, block]


def _na_bias_tables(rpb, rows):
    depth = rpb.shape[0]
    n_groups = rows // NA_QROWS
    n_off = 2 * NA_KH - 1
    qc = np.arange(GRID_W)
    cs = np.clip(qc - NA_KW // 2, 0, GRID_W - NA_KW)
    kc = np.arange(GRID_W)
    col_ok = (kc[None, :] >= cs[:, None]) & (kc[None, :] < cs[:, None] + NA_KW)
    dc_idx = np.clip(kc[None, :] - qc[:, None] + NA_KW - 1, 0, 2 * NA_KW - 2)
    dc_hot = jnp.asarray(np.eye(2 * NA_KW - 1, dtype=np.float32)[dc_idx])
    blocks = jnp.einsum("lhab,qcb->lhaqc", rpb, dc_hot, precision=lax.Precision.HIGHEST) * LOG2E
    blocks = jnp.where(jnp.asarray(col_ok)[None, None, None], blocks, -jnp.inf)
    ext = jnp.concatenate([blocks, jnp.full((depth, NA_HEADS, 1, GRID_W, GRID_W), -jnp.inf, F32)], axis=2)
    index = []
    for g in (0, 1, n_groups - 1):
        r = NA_QROWS * g + np.arange(NA_QROWS)
        base = int(np.clip(NA_QROWS * g - NA_KH // 2, 0, rows - NA_KROWS))
        krow = base + np.arange(NA_KROWS)
        rs = np.clip(r - NA_KH // 2, 0, rows - NA_KH)
        row_ok = (krow[None, :] >= rs[:, None]) & (krow[None, :] < rs[:, None] + NA_KH)
        dr_idx = krow[None, :] - r[:, None] + NA_KH - 1
        index.append(tuple(tuple(int(dr_idx[q, k]) if row_ok[q, k] else n_off for k in range(NA_KROWS))
                           for q in range(NA_QROWS)))
    tq, tk = NA_QROWS * GRID_W, NA_KROWS * GRID_W
    return pl.pallas_call(
        functools.partial(_bias_assemble_kernel, index=tuple(index)),
        grid=(depth, NA_HEADS),
        in_specs=[pl.BlockSpec((1, 1, n_off + 1, GRID_W, GRID_W), lambda l, h: (l, h, 0, 0, 0))],
        out_specs=pl.BlockSpec((1, len(index), 1, tq, tk), lambda l, h: (l, 0, h, 0, 0)),
        out_shape=jax.ShapeDtypeStruct((depth, len(index), NA_HEADS, tq, tk), F32),
        compiler_params=pltpu.CompilerParams(dimension_semantics=("parallel", "parallel")),
        name="na_bias",
    )(ext)


def _na_kernel(q_ref, k_ref, v_ref, *rest, rows):
    bias_refs, o_ref = rest[:NA_GROUPS], rest[NA_GROUPS]
    tq = NA_QROWS * GRID_W
    for u in range(NA_GROUPS):
        g = pl.program_id(1) * NA_GROUPS + u
        _na_group(q_ref.at[:, u * tq:(u + 1) * tq], k_ref, v_ref, bias_refs[u], o_ref.at[:, u * tq:(u + 1) * tq],
                  g, rows)


def _na_group(q_ref, k_ref, v_ref, bias_ref, o_ref, g, rows):
    start = pl.multiple_of(_na_key_base(g, rows) * GRID_W, GRID_W)
    nk = NA_KROWS * GRID_W
    tq = NA_QROWS * GRID_W
    lo_half = lax.broadcasted_iota(jnp.int32, (1, LANES), 1) < NA_HEAD_DIM
    outs = []
    for pair in range(NA_HEADS // 2):
        cols = slice(pair * LANES, (pair + 1) * LANES)
        qp = q_ref[0, :, cols]
        kp = k_ref[0, pl.ds(start, nk), cols]
        vp = v_ref[0, pl.ds(start, nk), cols]
        q2 = jnp.concatenate([jnp.where(lo_half, qp, 0), jnp.where(lo_half, 0, qp)], axis=0)
        s = lax.dot_general(q2, kp, (((1,), (1,)), ((), ())), preferred_element_type=F32)
        s = s + bias_ref[0, 2 * pair:2 * pair + 2].reshape(2 * tq, nk)
        e = jnp.exp2(s - jnp.max(s, axis=-1, keepdims=True))
        inv = 1.0 / jnp.sum(e, axis=-1, keepdims=True)
        e = e.astype(BF16)
        e2 = jnp.concatenate([e[:tq], e[tq:]], axis=1)
        v2 = jnp.concatenate([jnp.where(lo_half, vp, 0), jnp.where(lo_half, 0, vp)], axis=0)
        outs.append(_dot(e2, v2) * jnp.where(lo_half, inv[:tq], inv[tq:]))
    o_ref[0] = _rms(jnp.concatenate(outs, axis=-1))


def _na(qkv, bias, layer, batch, seq):
    rows = seq // GRID_W
    n_groups = rows // NA_QROWS
    tq = NA_QROWS * GRID_W
    tk = NA_KROWS * GRID_W

    def pattern(u):
        def index(b, s):
            g = s * NA_GROUPS + u
            return (layer, jnp.where(g == 0, 0, jnp.where(g == n_groups - 1, 2, 1)), 0, 0, 0)
        return index

    return pl.pallas_call(
        functools.partial(_na_kernel, rows=rows),
        grid=(batch, n_groups // NA_GROUPS),
        in_specs=[pl.BlockSpec((1, NA_GROUPS * tq, NA_WIDTH), lambda b, s: (b, s, 0)),
                  pl.BlockSpec((1, seq, NA_WIDTH), lambda b, s: (b, 0, 1)),
                  pl.BlockSpec((1, seq, NA_WIDTH), lambda b, s: (b, 0, 2))]
                 + [pl.BlockSpec((None, 1, NA_HEADS, tq, tk), pattern(u)) for u in range(NA_GROUPS)],
        out_specs=pl.BlockSpec((1, NA_GROUPS * tq, NA_WIDTH), lambda b, s: (b, s, 0)),
        out_shape=jax.ShapeDtypeStruct((batch, seq, NA_WIDTH), F32),
        compiler_params=pltpu.CompilerParams(dimension_semantics=("parallel", "arbitrary"),
                                             vmem_limit_bytes=VMEM_LIMIT),
        name="na",
    )(qkv, qkv, qkv, *([bias] * NA_GROUPS))


def _dft_mats(seq):
    half = seq // 2
    m = jnp.arange(half, dtype=jnp.int32)[None, :]
    a = jnp.arange(half // DFT_SPLIT, dtype=jnp.int32)[:, None]
    b = jnp.arange(DFT_SPLIT, dtype=jnp.int32)[:, None]
    ang_a = ((DFT_SPLIT * a * m) % seq).astype(F32) * (2.0 * math.pi / seq)
    ang_b = ((b * m) % seq).astype(F32) * (2.0 * math.pi / seq)
    ca, sa = jnp.cos(ang_a)[:, None, :], jnp.sin(ang_a)[:, None, :]
    cb, sb = jnp.cos(ang_b)[None, :, :], jnp.sin(ang_b)[None, :, :]
    cmat = (ca * cb - sa * sb).reshape(half, half)
    smat = (sa * cb + ca * sb).reshape(half, half)
    ang_t = jnp.arange(half, dtype=F32)[:, None] * (math.pi / seq)
    return cmat.astype(BF16), smat.astype(BF16), jnp.cos(ang_t), jnp.sin(ang_t)


def _filter_features(seq):
    bands = (FILTER_EMB - 1) // 2
    t = jnp.linspace(0.0, 1.0, seq, dtype=F32)[:, None]
    w = 2.0 * math.pi * jnp.arange(seq, dtype=F32)[:, None] / seq
    f = jnp.linspace(1e-4, bands - 1, bands, dtype=F32)[None, :]
    z = jnp.concatenate([t, jnp.cos(f * w), -jnp.sin(f * w)], axis=-1)
    return t, jnp.pad(z, ((0, 0), (0, LANES - FILTER_EMB)))


def _half_spectra(xe, xo, c_ref, s_ref, ct, st):
    nc = xe.shape[1]
    x = jnp.concatenate([xe, xo], axis=-1)
    rc = _dot(c_ref[...], x)
    rs = _dot(s_ref[...], x)
    ae, ao = rc[:, :nc], rc[:, nc:]
    be, bo = rs[:, :nc], rs[:, nc:]
    tr = ct * ao - st * bo
    ti = ct * bo + st * ao
    return ae + tr, -(be + ti), ae - tr, ti - be


def _dot3(a, b):
    a_hi = a.astype(BF16)
    a_lo = (a - a_hi.astype(F32)).astype(BF16)
    b_hi = b.astype(BF16)
    b_lo = (b - b_hi.astype(F32)).astype(BF16)
    return _dot(jnp.concatenate([a_hi, a_lo, a_hi], axis=1), jnp.concatenate([b_hi, b_hi, b_lo], axis=0))


def _alternating(n):
    row = lax.broadcasted_iota(jnp.int32, (n, 1), 0)
    return jnp.where((row & 1) == 0, 1.0, -1.0)


def _filt_kernel(z_ref, w1_ref, b1_ref, w2_ref, b2_ref, w3_ref, b3_ref, w4_ref, fr_ref, t_ref,
                 dl_ref, c_ref, s_ref, ct_ref, st_ref, kpr_ref, kpi_ref, kqr_ref, kqi_ref, kmid_ref):
    half = z_ref.shape[0] // 2
    n_fft = 4 * half
    hp = lax.Precision.HIGHEST
    fr = fr_ref[...]
    h = jnp.sin(fr * (jnp.dot(z_ref[...], w1_ref[...], precision=hp, preferred_element_type=F32) + b1_ref[...]))
    h = jnp.sin(fr * (jnp.dot(h, w2_ref[...], precision=hp, preferred_element_type=F32) + b2_ref[...]))
    h = jnp.sin(fr * (jnp.dot(h, w3_ref[...], precision=hp, preferred_element_type=F32) + b3_ref[...]))
    h = _dot3(h, w4_ref[...])
    decay = jnp.exp(-t_ref[...] * dl_ref[...])
    h = h * jnp.concatenate([decay, decay], axis=-1)
    he, ho = h[:half], h[half:]
    first = lax.broadcasted_iota(jnp.int32, (half, 1), 0) == 0
    hf0, hb0 = he[:, :HY_WIDTH], he[:, HY_WIDTH:]
    tot = jnp.sum(jnp.abs(he) + jnp.abs(ho), axis=0, keepdims=True)
    lag0 = jnp.sum(jnp.where(first, jnp.abs(hf0 + hb0) - jnp.abs(hf0) - jnp.abs(hb0), 0.0), axis=0, keepdims=True)
    inv = 1.0 / (tot[:, :HY_WIDTH] + tot[:, HY_WIDTH:] + lag0)

    ct, st = ct_ref[...], st_ref[...]
    fpr, fpi, fqr, fqi = _half_spectra(he[:, :HY_WIDTH].astype(BF16), ho[:, :HY_WIDTH].astype(BF16),
                                       c_ref, s_ref, ct, st)
    bpr, bpi, bqr, bqi = _half_spectra(he[:, HY_WIDTH:].astype(BF16), ho[:, HY_WIDTH:].astype(BF16),
                                       c_ref, s_ref, ct, st)
    scale = inv * jnp.where(first, 1.0 / n_fft, 2.0 / n_fft)
    kpr_ref[...] = (fpr + bpr) * scale
    kpi_ref[...] = (fpi - bpi) * scale
    kqr_ref[...] = (fqr + bqr) * scale
    kqi_ref[...] = (fqi - bqi) * scale
    alt = _alternating(half)
    a1 = jnp.sum(he * alt, axis=0, keepdims=True)
    b1 = jnp.sum(ho * alt, axis=0, keepdims=True)
    mid = inv * (2.0 / n_fft)
    kmid_ref[0:1, :] = (a1[:, :HY_WIDTH] + a1[:, HY_WIDTH:]) * mid
    kmid_ref[1:2, :] = (b1[:, HY_WIDTH:] - b1[:, :HY_WIDTH]) * mid


def _filter_spectrum(z, t, deltas, cmat, smat, ct, st, w1, b1, w2, b2, w3, b3, w4, freq):
    half = z.shape[0] // 2
    pad2 = lambda w, r, c: jnp.pad(w, ((0, r - w.shape[0]), (0, c - w.shape[1])))
    even_then_odd = lambda a: jnp.concatenate([a[0::2], a[1::2]], axis=0)
    args = (even_then_odd(z), pad2(w1, LANES, LANES), pad2(b1[None], 1, LANES), pad2(w2, LANES, LANES),
            pad2(b2[None], 1, LANES), pad2(w3, LANES, LANES), pad2(b3[None], 1, LANES),
            pad2(w4, LANES, 2 * HY_WIDTH), pad2(freq[None], 1, LANES), even_then_odd(t), deltas, cmat, smat, ct, st)
    spec = jax.ShapeDtypeStruct((half, HY_WIDTH), F32)
    return pl.pallas_call(
        _filt_kernel,
        out_shape=[spec, spec, spec, spec, jax.ShapeDtypeStruct((2, HY_WIDTH), F32)],
        compiler_params=pltpu.CompilerParams(vmem_limit_bytes=VMEM_LIMIT),
        name="filt",
    )(*args)


def _short_conv(u_ref, w_ref, b_ref):
    half = u_ref.shape[0] // 2
    row = lax.broadcasted_iota(jnp.int32, (half, 1), 0)
    ue = u_ref[pl.ds(0, half, stride=2), :]
    uo = u_ref[pl.ds(1, half, stride=2), :]
    uo_prev = jnp.where(row == 0, 0.0, pltpu.roll(uo, 1, 0))
    ue_next = jnp.where(row == half - 1, 0.0, pltpu.roll(ue, half - 1, 0))
    w0, w1, w2, b = w_ref[0:1, :], w_ref[1:2, :], w_ref[2:3, :], b_ref[...]
    return uo_prev * w0 + ue * w1 + uo * w2 + b, ue * w0 + uo * w1 + ue_next * w2 + b


def _hyena_tile(x0_ref, x1_ref, hv_ref, w0_ref, w1_ref, wv_ref, b0_ref, b1_ref, bv_ref,
                kpr_ref, kpi_ref, kqr_ref, kqi_ref, kmid_ref, hb_ref, c_ref, s_ref, ct_ref, st_ref, o_ref):
    half = x0_ref.shape[0] // 2
    nc = x0_ref.shape[1]
    ct, st = ct_ref[...], st_ref[...]
    hve, hvo = _short_conv(hv_ref, wv_ref, bv_ref)
    x1e, x1o = _short_conv(x1_ref, w1_ref, b1_ref)
    ve, vo = hve * x1e, hvo * x1o
    pr, pi, qr, qi = _half_spectra(ve.astype(BF16), vo.astype(BF16), c_ref, s_ref, ct, st)
    kpr, kpi, kqr, kqi = kpr_ref[...], kpi_ref[...], kqr_ref[...], kqi_ref[...]
    ypr, ypi = pr * kpr - pi * kpi, pr * kpi + pi * kpr
    yqr, yqi = qr * kqr - qi * kqi, qr * kqi + qi * kqr
    dr, di = ypr - yqr, ypi - yqi
    zr = jnp.concatenate([(ypr + yqr).astype(BF16), (dr * ct - di * st).astype(BF16)], axis=-1)
    zi = jnp.concatenate([(ypi + yqi).astype(BF16), (dr * st + di * ct).astype(BF16)], axis=-1)
    y = _dot(c_ref[...], zr) - _dot(s_ref[...], zi)
    alt = _alternating(half)
    a1 = jnp.sum(ve * alt, axis=0, keepdims=True)
    b1 = jnp.sum(vo * alt, axis=0, keepdims=True)
    kmr, kmi = kmid_ref[0:1, :], kmid_ref[1:2, :]
    ye = y[:, :nc] + alt * (a1 * kmr + b1 * kmi)
    yo = y[:, nc:] - alt * (a1 * kmi - b1 * kmr)
    x0e, x0o = _short_conv(x0_ref, w0_ref, b0_ref)
    hb = hb_ref[...]
    o_ref[pl.ds(0, half, stride=2), :] = x0e * (ye + ve * hb)
    o_ref[pl.ds(1, half, stride=2), :] = x0o * (yo + vo * hb)


HY_TILE_ARGS = 15


def _hyena_kernel(*refs):
    n = HY_SUB * HY_TILE_ARGS
    shared, o_ref, tile_out = refs[n:n + 4], refs[n + 4], refs[n + 5:]
    for t in range(HY_SUB):
        _hyena_tile(*refs[t * HY_TILE_ARGS:(t + 1) * HY_TILE_ARGS], *shared, tile_out[t])
        o_ref[:, t * HY_CTILE:(t + 1) * HY_CTILE] = tile_out[t][...]


def _hyena(hy, w_sc, b_sc, cmat, smat, ct, st, kspec, hy_bias, batch, seq):
    kpr, kpi, kqr, kqi, kmid = kspec
    nt = HY_WIDTH // HY_CTILE
    half = seq // 2
    in_specs, args = [], []
    for t in range(HY_SUB):
        def tile_spec(shape, off, mode=None, t=t):
            lead = (lambda b: (b, 0)) if len(shape) == 3 else (lambda b: (0,))
            return pl.BlockSpec(shape, lambda j, b: lead(b) + (off * nt + j * HY_SUB + t,), pipeline_mode=mode)

        data = lambda off: tile_spec((None, seq, HY_CTILE), off)
        rows_of = lambda r, off=0: tile_spec((r, HY_CTILE), off)
        spec = tile_spec((half, HY_CTILE), 0, pl.Buffered(1))
        in_specs += [data(0), data(1), data(2), rows_of(3, 0), rows_of(3, 1), rows_of(3, 2),
                     rows_of(1, 0), rows_of(1, 1), rows_of(1, 2), spec, spec, spec, spec, rows_of(2), rows_of(1)]
        args += [hy, hy, hy, w_sc, w_sc, w_sc, b_sc, b_sc, b_sc, kpr, kpi, kqr, kqi, kmid, hy_bias]
    in_specs += [_const_spec(cmat.shape), _const_spec(smat.shape), _const_spec(ct.shape), _const_spec(st.shape)]
    args += [cmat, smat, ct, st]
    return pl.pallas_call(
        _hyena_kernel,
        grid=(nt // HY_SUB, batch),
        in_specs=in_specs,
        out_specs=pl.BlockSpec((None, seq, HY_SUB * HY_CTILE), lambda j, b: (b, 0, j)),
        out_shape=jax.ShapeDtypeStruct((batch, seq, HY_WIDTH), F32),
        scratch_shapes=[pltpu.VMEM((seq, HY_CTILE), F32) for _ in range(HY_SUB)],
        compiler_params=pltpu.CompilerParams(dimension_semantics=("arbitrary", "arbitrary"),
                                             vmem_limit_bytes=VMEM_LIMIT),
        name="hyena",
    )(*args)


def kernel(x, p, g_ffa, w_ffa_gate, w_ffa_up, w_ffa_down, g_mix, w_in, na_rpb, w_sc, b_sc, w_f1, b_f1, w_f2, b_f2, w_f3, b_f3, w_f4, filt_freq, hy_bias, g_out, w_out, g_ffb, w_ffb_gate, w_ffb_up, w_ffb_down, g_ple, w_ple_gate, w_ple_proj, g_final):
    batch, seq, _ = x.shape
    depth = p.shape[0]
    m = batch * seq
    rows = seq // GRID_W
    bf = lambda w: w.astype(BF16)
    vec = lambda g: g[:, None, :]

    cmat, smat, ct, st = _dft_mats(seq)
    t, z = _filter_features(seq)
    deltas = jnp.abs(jnp.linspace(MIN_DECAY, MAX_DECAY, HY_WIDTH, dtype=F32))[None, :]

    in_params = (vec(g_ffa), bf(w_ffa_gate), bf(w_ffa_up), bf(w_ffa_down), vec(g_mix), bf(w_in))
    out_params = (vec(g_out), bf(w_out), vec(g_ffb), bf(w_ffb_gate), bf(w_ffb_up), bf(w_ffb_down),
                  vec(g_ple), bf(w_ple_gate), bf(w_ple_proj))
    pr = p.reshape(depth, m, PLE_DIM)

    bias = _na_bias_tables(na_rpb, rows)

    xr = x.reshape(m, D_MODEL)
    for i in range(depth):
        xr, qkv, hy = _rows_in(xr, i, *in_params)
        yna = _na(qkv.reshape(batch, seq, 3 * NA_WIDTH), bias, i, batch, seq)
        kspec = _filter_spectrum(z, t, deltas, cmat, smat, ct, st, w_f1[i], b_f1[i], w_f2[i], b_f2[i],
                                 w_f3[i], b_f3[i], w_f4[i], filt_freq[i])
        yhy = _hyena(hy.reshape(batch, seq, 3 * HY_WIDTH), w_sc[i], b_sc[i][None], cmat, smat, ct, st, kspec,
                     hy_bias[i][None], batch, seq)
        xr = _rows_out(xr, yna.reshape(m, NA_WIDTH), yhy.reshape(m, HY_WIDTH), pr, i, *out_params,
                       g_final[None], final=(i == depth - 1))
    return xr.reshape(batch, seq, D_MODEL)
```

```python
import functools
import math

import numpy as np
import jax
import jax.numpy as jnp
from jax import lax
from jax.experimental import pallas as pl
from jax.experimental.pallas import tpu as pltpu

F32 = jnp.float32
BF16 = jnp.bfloat16

D_MODEL = 1024
GRID_W = 64
NA_HEADS = 8
NA_HEAD_DIM = 64
NA_WIDTH = NA_HEADS * NA_HEAD_DIM
NA_KH = 8
NA_KW = 16
HY_WIDTH = D_MODEL - NA_WIDTH
FILTER_EMB = 33
DECAY_TARGET = 1e-2
FAST_DECAY_PCT = 0.3
SLOW_DECAY_PCT = 1.5
MAX_DECAY = math.log(DECAY_TARGET) / FAST_DECAY_PCT
MIN_DECAY = math.log(DECAY_TARGET) / SLOW_DECAY_PCT
D_FF = 2816
PLE_DIM = 256
EPS = 1e-6

LANES = 128
ROW_TILE = 512
LOG2E = math.log2(math.e)
NA_QSCALE = NA_HEAD_DIM ** -0.5 * LOG2E
FF_CHUNKS = ((0, 1024), (1024, 2048), (2048, D_FF))
NA_QROWS = 4
NA_GROUPS = 2
NA_KROWS = 12
HY_CTILE = 128
HY_SUB = 2
DFT_SPLIT = 64
VMEM_LIMIT = 56 * 1024 * 1024


def _rms(x):
    return x * lax.rsqrt(jnp.mean(x * x, axis=-1, keepdims=True) + EPS)


def _dot(a, b):
    return jnp.dot(a, b, preferred_element_type=F32)


def _swiglu_residual(x, g, wg_ref, wu_ref, wd_ref):
    h = (_rms(x) * g).astype(BF16)
    acc = None
    for lo, hi in FF_CHUNKS:
        gate = _dot(h, wg_ref[:, lo:hi])
        up = _dot(h, wu_ref[:, lo:hi])
        act = (gate * jax.nn.sigmoid(gate) * up).astype(BF16)
        part = _dot(act, wd_ref[lo:hi, :])
        acc = part if acc is None else acc + part
    return x + 0.5 * acc


def _const_spec(shape):
    nd = len(shape)
    return pl.BlockSpec(shape, lambda *_: (0,) * nd, pipeline_mode=pl.Buffered(1))


def _layer_spec(stacked, layer):
    return pl.BlockSpec((None,) + stacked.shape[1:], lambda *_: (layer, 0, 0), pipeline_mode=pl.Buffered(1))


def _row_spec(width):
    return pl.BlockSpec((ROW_TILE, width), lambda i: (i, 0))


def _rows_in_kernel(x_ref, g_ffa_ref, wg_ref, wu_ref, wd_ref, g_mix_ref, w_in_ref,
                    x_out_ref, qkv_ref, hy_ref):
    x = _swiglu_residual(x_ref[...], g_ffa_ref[...], wg_ref, wu_ref, wd_ref)
    x_out_ref[...] = x
    h = (_rms(x) * g_mix_ref[...]).astype(BF16)
    nq = 3 * NA_WIDTH
    qkv = _dot(h, w_in_ref[:, :nq])
    qkv_ref[:, :NA_WIDTH] = (qkv[:, :NA_WIDTH] * NA_QSCALE).astype(BF16)
    qkv_ref[:, NA_WIDTH:] = qkv[:, NA_WIDTH:].astype(BF16)
    hy_ref[...] = _dot(h, w_in_ref[:, nq:])


def _rows_in(x, layer, g_ffa, wg, wu, wd, g_mix, w_in):
    m = x.shape[0]
    return pl.pallas_call(
        _rows_in_kernel,
        grid=(m // ROW_TILE,),
        in_specs=[_row_spec(D_MODEL)] + [_layer_spec(c, layer) for c in (g_ffa, wg, wu, wd, g_mix, w_in)],
        out_specs=[_row_spec(D_MODEL), _row_spec(3 * NA_WIDTH), _row_spec(3 * HY_WIDTH)],
        out_shape=[jax.ShapeDtypeStruct((m, D_MODEL), F32),
                   jax.ShapeDtypeStruct((m, 3 * NA_WIDTH), BF16),
                   jax.ShapeDtypeStruct((m, 3 * HY_WIDTH), F32)],
        compiler_params=pltpu.CompilerParams(dimension_semantics=("parallel",), vmem_limit_bytes=VMEM_LIMIT),
        name="rows_in",
    )(x, g_ffa, wg, wu, wd, g_mix, w_in)


def _rows_out_kernel(x_ref, yna_ref, yhy_ref, p_ref, g_out_ref, w_out_ref, g_ffb_ref, wg_ref, wu_ref, wd_ref,
                     g_ple_ref, w_pg_ref, w_pp_ref, g_fin_ref, o_ref, *, final):
    g_out = g_out_ref[...]
    y = jnp.concatenate([(yna_ref[...] * g_out[:, :NA_WIDTH]).astype(BF16),
                         (_rms(yhy_ref[...]) * g_out[:, NA_WIDTH:]).astype(BF16)], axis=-1)
    x = x_ref[...] + _dot(y, w_out_ref[...])
    x = _swiglu_residual(x, g_ffb_ref[...], wg_ref, wu_ref, wd_ref)
    h = (_rms(x) * g_ple_ref[...]).astype(BF16)
    gate = jax.nn.sigmoid(_dot(h, w_pg_ref[...]))
    x = x + gate * _dot(p_ref[...].astype(BF16), w_pp_ref[...])
    if final:
        x = _rms(x) * g_fin_ref[...]
    o_ref[...] = x


def _rows_out(x, yna, yhy, p, layer, g_out, w_out, g_ffb, wg, wu, wd, g_ple, w_pg, w_pp, g_fin, final):
    m = x.shape[0]
    stacks = (g_out, w_out, g_ffb, wg, wu, wd, g_ple, w_pg, w_pp)
    return pl.pallas_call(
        functools.partial(_rows_out_kernel, final=final),
        grid=(m // ROW_TILE,),
        in_specs=[_row_spec(D_MODEL), _row_spec(NA_WIDTH), _row_spec(HY_WIDTH),
                  pl.BlockSpec((None, ROW_TILE, PLE_DIM), lambda i: (layer, i, 0))]
                 + [_layer_spec(c, layer) for c in stacks] + [_const_spec(g_fin.shape)],
        out_specs=_row_spec(D_MODEL),
        out_shape=jax.ShapeDtypeStruct((m, D_MODEL), F32),
        compiler_params=pltpu.CompilerParams(dimension_semantics=("parallel",), vmem_limit_bytes=VMEM_LIMIT),
        name="rows_out",
    )(x, yna, yhy, p, *stacks, g_fin)


def _na_key_base(g, rows):
    return jnp.clip(NA_QROWS * g - NA_KH // 2, 0, rows - NA_KROWS)


def _bias_assemble_kernel(ext_ref, o_ref, *, index):
    for p, per_q in enumerate(index):
        for q, per_k in enumerate(per_q):
            for k, block in enumerate(per_k):
                o_ref[0, p, 0, q * GRID_W:(q + 1) * GRID_W, k * GRID_W:(k + 1) * GRID_W] = ext_ref[0, 0, block]


def _na_bias_tables(rpb, rows):
    depth = rpb.shape[0]
    n_groups = rows // NA_QROWS
    n_off = 2 * NA_KH - 1
    qc = np.arange(GRID_W)
    cs = np.clip(qc - NA_KW // 2, 0, GRID_W - NA_KW)
    kc = np.arange(GRID_W)
    col_ok = (kc[None, :] >= cs[:, None]) & (kc[None, :] < cs[:, None] + NA_KW)
    dc_idx = np.clip(kc[None, :] - qc[:, None] + NA_KW - 1, 0, 2 * NA_KW - 2)
    dc_hot = jnp.asarray(np.eye(2 * NA_KW - 1, dtype=np.float32)[dc_idx])
    blocks = jnp.einsum("lhab,qcb->lhaqc", rpb, dc_hot, precision=lax.Precision.HIGHEST) * LOG2E
    blocks = jnp.where(jnp.asarray(col_ok)[None, None, None], blocks, -jnp.inf)
    ext = jnp.concatenate([blocks, jnp.full((depth, NA_HEADS, 1, GRID_W, GRID_W), -jnp.inf, F32)], axis=2)
    index = []
    for g in (0, 1, n_groups - 1):
        r = NA_QROWS * g + np.arange(NA_QROWS)
        base = int(np.clip(NA_QROWS * g - NA_KH // 2, 0, rows - NA_KROWS))
        krow = base + np.arange(NA_KROWS)
        rs = np.clip(r - NA_KH // 2, 0, rows - NA_KH)
        row_ok = (krow[None, :] >= rs[:, None]) & (krow[None, :] < rs[:, None] + NA_KH)
        dr_idx = krow[None, :] - r[:, None] + NA_KH - 1
        index.append(tuple(tuple(int(dr_idx[q, k]) if row_ok[q, k] else n_off for k in range(NA_KROWS))
                           for q in range(NA_QROWS)))
    tq, tk = NA_QROWS * GRID_W, NA_KROWS * GRID_W
    return pl.pallas_call(
        functools.partial(_bias_assemble_kernel, index=tuple(index)),
        grid=(depth, NA_HEADS),
        in_specs=[pl.BlockSpec((1, 1, n_off + 1, GRID_W, GRID_W), lambda l, h: (l, h, 0, 0, 0))],
        out_specs=pl.BlockSpec((1, len(index), 1, tq, tk), lambda l, h: (l, 0, h, 0, 0)),
        out_shape=jax.ShapeDtypeStruct((depth, len(index), NA_HEADS, tq, tk), F32),
        compiler_params=pltpu.CompilerParams(dimension_semantics=("parallel", "parallel")),
        name="na_bias",
    )(ext)


def _na_kernel(q_ref, k_ref, v_ref, *rest, rows):
    bias_refs, o_ref = rest[:NA_GROUPS], rest[NA_GROUPS]
    tq = NA_QROWS * GRID_W
    for u in range(NA_GROUPS):
        g = pl.program_id(1) * NA_GROUPS + u
        _na_group(q_ref.at[:, u * tq:(u + 1) * tq], k_ref, v_ref, bias_refs[u], o_ref.at[:, u * tq:(u + 1) * tq],
                  g, rows)


def _na_group(q_ref, k_ref, v_ref, bias_ref, o_ref, g, rows):
    start = pl.multiple_of(_na_key_base(g, rows) * GRID_W, GRID_W)
    nk = NA_KROWS * GRID_W
    tq = NA_QROWS * GRID_W
    lo_half = lax.broadcasted_iota(jnp.int32, (1, LANES), 1) < NA_HEAD_DIM
    outs = []
    for pair in range(NA_HEADS // 2):
        cols = slice(pair * LANES, (pair + 1) * LANES)
        qp = q_ref[0, :, cols]
        kp = k_ref[0, pl.ds(start, nk), cols]
        vp = v_ref[0, pl.ds(start, nk), cols]
        q2 = jnp.concatenate([jnp.where(lo_half, qp, 0), jnp.where(lo_half, 0, qp)], axis=0)
        s = lax.dot_general(q2, kp, (((1,), (1,)), ((), ())), preferred_element_type=F32)
        s = s + bias_ref[0, 2 * pair:2 * pair + 2].reshape(2 * tq, nk)
        e = jnp.exp2(s - jnp.max(s, axis=-1, keepdims=True))
        inv = 1.0 / jnp.sum(e, axis=-1, keepdims=True)
        e = e.astype(BF16)
        e2 = jnp.concatenate([e[:tq], e[tq:]], axis=1)
        v2 = jnp.concatenate([jnp.where(lo_half, vp, 0), jnp.where(lo_half, 0, vp)], axis=0)
        outs.append(_dot(e2, v2) * jnp.where(lo_half, inv[:tq], inv[tq:]))
    o_ref[0] = _rms(jnp.concatenate(outs, axis=-1))


def _na(qkv, bias, layer, batch, seq):
    rows = seq // GRID_W
    n_groups = rows // NA_QROWS
    tq = NA_QROWS * GRID_W
    tk = NA_KROWS * GRID_W

    def pattern(u):
        def index(b, s):
            g = s * NA_GROUPS + u
            return (layer, jnp.where(g == 0, 0, jnp.where(g == n_groups - 1, 2, 1)), 0, 0, 0)
        return index

    return pl.pallas_call(
        functools.partial(_na_kernel, rows=rows),
        grid=(batch, n_groups // NA_GROUPS),
        in_specs=[pl.BlockSpec((1, NA_GROUPS * tq, NA_WIDTH), lambda b, s: (b, s, 0)),
                  pl.BlockSpec((1, seq, NA_WIDTH), lambda b, s: (b, 0, 1)),
                  pl.BlockSpec((1, seq, NA_WIDTH), lambda b, s: (b, 0, 2))]
                 + [pl.BlockSpec((None, 1, NA_HEADS, tq, tk), pattern(u)) for u in range(NA_GROUPS)],
        out_specs=pl.BlockSpec((1, NA_GROUPS * tq, NA_WIDTH), lambda b, s: (b, s, 0)),
        out_shape=jax.ShapeDtypeStruct((batch, seq, NA_WIDTH), F32),
        compiler_params=pltpu.CompilerParams(dimension_semantics=("parallel", "arbitrary"),
                                             vmem_limit_bytes=VMEM_LIMIT),
        name="na",
    )(qkv, qkv, qkv, *([bias] * NA_GROUPS))


def _dft_mats(seq):
    half = seq // 2
    m = jnp.arange(half, dtype=jnp.int32)[None, :]
    a = jnp.arange(half // DFT_SPLIT, dtype=jnp.int32)[:, None]
    b = jnp.arange(DFT_SPLIT, dtype=jnp.int32)[:, None]
    ang_a = ((DFT_SPLIT * a * m) % seq).astype(F32) * (2.0 * math.pi / seq)
    ang_b = ((b * m) % seq).astype(F32) * (2.0 * math.pi / seq)
    ca, sa = jnp.cos(ang_a)[:, None, :], jnp.sin(ang_a)[:, None, :]
    cb, sb = jnp.cos(ang_b)[None, :, :], jnp.sin(ang_b)[None, :, :]
    cmat = (ca * cb - sa * sb).reshape(half, half)
    smat = (sa * cb + ca * sb).reshape(half, half)
    ang_t = jnp.arange(half, dtype=F32)[:, None] * (math.pi / seq)
    return cmat.astype(BF16), smat.astype(BF16), jnp.cos(ang_t), jnp.sin(ang_t)


def _filter_features(seq):
    bands = (FILTER_EMB - 1) // 2
    t = jnp.linspace(0.0, 1.0, seq, dtype=F32)[:, None]
    w = 2.0 * math.pi * jnp.arange(seq, dtype=F32)[:, None] / seq
    f = jnp.linspace(1e-4, bands - 1, bands, dtype=F32)[None, :]
    z = jnp.concatenate([t, jnp.cos(f * w), -jnp.sin(f * w)], axis=-1)
    return t, jnp.pad(z, ((0, 0), (0, LANES - FILTER_EMB)))


def _half_spectra(xe, xo, c_ref, s_ref, ct, st):
    nc = xe.shape[1]
    x = jnp.concatenate([xe, xo], axis=-1)
    rc = _dot(c_ref[...], x)
    rs = _dot(s_ref[...], x)
    ae, ao = rc[:, :nc], rc[:, nc:]
    be, bo = rs[:, :nc], rs[:, nc:]
    tr = ct * ao - st * bo
    ti = ct * bo + st * ao
    return ae + tr, -(be + ti), ae - tr, ti - be


def _dot3(a, b):
    a_hi = a.astype(BF16)
    a_lo = (a - a_hi.astype(F32)).astype(BF16)
    b_hi = b.astype(BF16)
    b_lo = (b - b_hi.astype(F32)).astype(BF16)
    return _dot(jnp.concatenate([a_hi, a_lo, a_hi], axis=1), jnp.concatenate([b_hi, b_hi, b_lo], axis=0))


def _alternating(n):
    row = lax.broadcasted_iota(jnp.int32, (n, 1), 0)
    return jnp.where((row & 1) == 0, 1.0, -1.0)


def _filt_kernel(z_ref, w1_ref, b1_ref, w2_ref, b2_ref, w3_ref, b3_ref, w4_ref, fr_ref, t_ref,
                 dl_ref, c_ref, s_ref, ct_ref, st_ref, kpr_ref, kpi_ref, kqr_ref, kqi_ref, kmid_ref):
    half = z_ref.shape[0] // 2
    n_fft = 4 * half
    fr = fr_ref[...]
    h = jnp.sin(fr * (_dot3(z_ref[...], w1_ref[...]) + b1_ref[...]))
    h = jnp.sin(fr * (_dot3(h, w2_ref[...]) + b2_ref[...]))
    h = jnp.sin(fr * (_dot3(h, w3_ref[...]) + b3_ref[...]))
    h = _dot3(h, w4_ref[...])
    decay = jnp.exp(-t_ref[...] * dl_ref[...])
    h = h * jnp.concatenate([decay, decay], axis=-1)
    he, ho = h[:half], h[half:]
    first = lax.broadcasted_iota(jnp.int32, (half, 1), 0) == 0
    hf0, hb0 = he[:, :HY_WIDTH], he[:, HY_WIDTH:]
    tot = jnp.sum(jnp.abs(he) + jnp.abs(ho), axis=0, keepdims=True)
    lag0 = jnp.sum(jnp.where(first, jnp.abs(hf0 + hb0) - jnp.abs(hf0) - jnp.abs(hb0), 0.0), axis=0, keepdims=True)
    inv = 1.0 / (tot[:, :HY_WIDTH] + tot[:, HY_WIDTH:] + lag0)

    ct, st = ct_ref[...], st_ref[...]
    fpr, fpi, fqr, fqi = _half_spectra(he[:, :HY_WIDTH].astype(BF16), ho[:, :HY_WIDTH].astype(BF16),
                                       c_ref, s_ref, ct, st)
    bpr, bpi, bqr, bqi = _half_spectra(he[:, HY_WIDTH:].astype(BF16), ho[:, HY_WIDTH:].astype(BF16),
                                       c_ref, s_ref, ct, st)
    scale = inv * jnp.where(first, 1.0 / n_fft, 2.0 / n_fft)
    kpr_ref[...] = (fpr + bpr) * scale
    kpi_ref[...] = (fpi - bpi) * scale
    kqr_ref[...] = (fqr + bqr) * scale
    kqi_ref[...] = (fqi - bqi) * scale
    alt = _alternating(half)
    a1 = jnp.sum(he * alt, axis=0, keepdims=True)
    b1 = jnp.sum(ho * alt, axis=0, keepdims=True)
    mid = inv * (2.0 / n_fft)
    kmid_ref[0:1, :] = (a1[:, :HY_WIDTH] + a1[:, HY_WIDTH:]) * mid
    kmid_ref[1:2, :] = (b1[:, HY_WIDTH:] - b1[:, :HY_WIDTH]) * mid


def _filter_spectrum(z, t, deltas, cmat, smat, ct, st, w1, b1, w2, b2, w3, b3, w4, freq):
    half = z.shape[0] // 2
    pad2 = lambda w, r, c: jnp.pad(w, ((0, r - w.shape[0]), (0, c - w.shape[1])))
    even_then_odd = lambda a: jnp.concatenate([a[0::2], a[1::2]], axis=0)
    args = (even_then_odd(z), pad2(w1, LANES, LANES), pad2(b1[None], 1, LANES), pad2(w2, LANES, LANES),
            pad2(b2[None], 1, LANES), pad2(w3, LANES, LANES), pad2(b3[None], 1, LANES),
            pad2(w4, LANES, 2 * HY_WIDTH), pad2(freq[None], 1, LANES), even_then_odd(t), deltas, cmat, smat, ct, st)
    spec = jax.ShapeDtypeStruct((half, HY_WIDTH), F32)
    return pl.pallas_call(
        _filt_kernel,
        out_shape=[spec, spec, spec, spec, jax.ShapeDtypeStruct((2, HY_WIDTH), F32)],
        compiler_params=pltpu.CompilerParams(vmem_limit_bytes=VMEM_LIMIT),
        name="filt",
    )(*args)


def _short_conv(u_ref, w_ref, b_ref):
    half = u_ref.shape[0] // 2
    row = lax.broadcasted_iota(jnp.int32, (half, 1), 0)
    ue = u_ref[pl.ds(0, half, stride=2), :]
    uo = u_ref[pl.ds(1, half, stride=2), :]
    uo_prev = jnp.where(row == 0, 0.0, pltpu.roll(uo, 1, 0))
    ue_next = jnp.where(row == half - 1, 0.0, pltpu.roll(ue, half - 1, 0))
    w0, w1, w2, b = w_ref[0:1, :], w_ref[1:2, :], w_ref[2:3, :], b_ref[...]
    return uo_prev * w0 + ue * w1 + uo * w2 + b, ue * w0 + uo * w1 + ue_next * w2 + b


def _hyena_tile(x0_ref, x1_ref, hv_ref, w0_ref, w1_ref, wv_ref, b0_ref, b1_ref, bv_ref,
                kpr_ref, kpi_ref, kqr_ref, kqi_ref, kmid_ref, hb_ref, c_ref, s_ref, ct_ref, st_ref, o_ref):
    half = x0_ref.shape[0] // 2
    nc = x0_ref.shape[1]
    ct, st = ct_ref[...], st_ref[...]
    hve, hvo = _short_conv(hv_ref, wv_ref, bv_ref)
    x1e, x1o = _short_conv(x1_ref, w1_ref, b1_ref)
    ve, vo = hve * x1e, hvo * x1o
    pr, pi, qr, qi = _half_spectra(ve.astype(BF16), vo.astype(BF16), c_ref, s_ref, ct, st)
    kpr, kpi, kqr, kqi = kpr_ref[...], kpi_ref[...], kqr_ref[...], kqi_ref[...]
    ypr, ypi = pr * kpr - pi * kpi, pr * kpi + pi * kpr
    yqr, yqi = qr * kqr - qi * kqi, qr * kqi + qi * kqr
    dr, di = ypr - yqr, ypi - yqi
    zr = jnp.concatenate([(ypr + yqr).astype(BF16), (dr * ct - di * st).astype(BF16)], axis=-1)
    zi = jnp.concatenate([(ypi + yqi).astype(BF16), (dr * st + di * ct).astype(BF16)], axis=-1)
    y = _dot(c_ref[...], zr) - _dot(s_ref[...], zi)
    alt = _alternating(half)
    a1 = jnp.sum(ve * alt, axis=0, keepdims=True)
    b1 = jnp.sum(vo * alt, axis=0, keepdims=True)
    kmr, kmi = kmid_ref[0:1, :], kmid_ref[1:2, :]
    ye = y[:, :nc] + alt * (a1 * kmr + b1 * kmi)
    yo = y[:, nc:] - alt * (a1 * kmi - b1 * kmr)
    x0e, x0o = _short_conv(x0_ref, w0_ref, b0_ref)
    hb = hb_ref[...]
    o_ref[pl.ds(0, half, stride=2), :] = x0e * (ye + ve * hb)
    o_ref[pl.ds(1, half, stride=2), :] = x0o * (yo + vo * hb)


HY_TILE_ARGS = 15


def _hyena_kernel(*refs):
    n = HY_SUB * HY_TILE_ARGS
    shared, o_ref, tile_out = refs[n:n + 4], refs[n + 4], refs[n + 5:]
    for t in range(HY_SUB):
        _hyena_tile(*refs[t * HY_TILE_ARGS:(t + 1) * HY_TILE_ARGS], *shared, tile_out[t])
        o_ref[:, t * HY_CTILE:(t + 1) * HY_CTILE] = tile_out[t][...]


def _hyena(hy, w_sc, b_sc, cmat, smat, ct, st, kspec, hy_bias, batch, seq):
    kpr, kpi, kqr, kqi, kmid = kspec
    nt = HY_WIDTH // HY_CTILE
    half = seq // 2
    in_specs, args = [], []
    for t in range(HY_SUB):
        def tile_spec(shape, off, mode=None, t=t):
            lead = (lambda b: (b, 0)) if len(shape) == 3 else (lambda b: (0,))
            return pl.BlockSpec(shape, lambda j, b: lead(b) + (off * nt + j * HY_SUB + t,), pipeline_mode=mode)

        data = lambda off: tile_spec((None, seq, HY_CTILE), off)
        rows_of = lambda r, off=0: tile_spec((r, HY_CTILE), off)
        spec = tile_spec((half, HY_CTILE), 0, pl.Buffered(1))
        in_specs += [data(0), data(1), data(2), rows_of(3, 0), rows_of(3, 1), rows_of(3, 2),
                     rows_of(1, 0), rows_of(1, 1), rows_of(1, 2), spec, spec, spec, spec, rows_of(2), rows_of(1)]
        args += [hy, hy, hy, w_sc, w_sc, w_sc, b_sc, b_sc, b_sc, kpr, kpi, kqr, kqi, kmid, hy_bias]
    in_specs += [_const_spec(cmat.shape), _const_spec(smat.shape), _const_spec(ct.shape), _const_spec(st.shape)]
    args += [cmat, smat, ct, st]
    return pl.pallas_call(
        _hyena_kernel,
        grid=(nt // HY_SUB, batch),
        in_specs=in_specs,
        out_specs=pl.BlockSpec((None, seq, HY_SUB * HY_CTILE), lambda j, b: (b, 0, j)),
        out_shape=jax.ShapeDtypeStruct((batch, seq, HY_WIDTH), F32),
        scratch_shapes=[pltpu.VMEM((seq, HY_CTILE), F32) for _ in range(HY_SUB)],
        compiler_params=pltpu.CompilerParams(dimension_semantics=("arbitrary", "arbitrary"),
                                             vmem_limit_bytes=VMEM_LIMIT),
        name="hyena",
    )(*args)


def kernel(x, p, g_ffa, w_ffa_gate, w_ffa_up, w_ffa_down, g_mix, w_in, na_rpb, w_sc, b_sc, w_f1, b_f1, w_f2, b_f2, w_f3, b_f3, w_f4, filt_freq, hy_bias, g_out, w_out, g_ffb, w_ffb_gate, w_ffb_up, w_ffb_down, g_ple, w_ple_gate, w_ple_proj, g_final):
    batch, seq, _ = x.shape
    depth = p.shape[0]
    m = batch * seq
    rows = seq // GRID_W
    bf = lambda w: w.astype(BF16)
    vec = lambda g: g[:, None, :]

    cmat, smat, ct, st = _dft_mats(seq)
    t, z = _filter_features(seq)
    deltas = jnp.abs(jnp.linspace(MIN_DECAY, MAX_DECAY, HY_WIDTH, dtype=F32))[None, :]

    in_params = (vec(g_ffa), bf(w_ffa_gate), bf(w_ffa_up), bf(w_ffa_down), vec(g_mix), bf(w_in))
    out_params = (vec(g_out), bf(w_out), vec(g_ffb), bf(w_ffb_gate), bf(w_ffb_up), bf(w_ffb_down),
                  vec(g_ple), bf(w_ple_gate), bf(w_ple_proj))
    pr = p.reshape(depth, m, PLE_DIM)

    bias = _na_bias_tables(na_rpb, rows)

    xr = x.reshape(m, D_MODEL)
    for i in range(depth):
        xr, qkv, hy = _rows_in(xr, i, *in_params)
        yna = _na(qkv.reshape(batch, seq, 3 * NA_WIDTH), bias, i, batch, seq)
        kspec = _filter_spectrum(z, t, deltas, cmat, smat, ct, st, w_f1[i], b_f1[i], w_f2[i], b_f2[i],
                                 w_f3[i], b_f3[i], w_f4[i], filt_freq[i])
        yhy = _hyena(hy.reshape(batch, seq, 3 * HY_WIDTH), w_sc[i], b_sc[i][None], cmat, smat, ct, st, kspec,
                     hy_bias[i][None], batch, seq)
        xr = _rows_out(xr, yna.reshape(m, NA_WIDTH), yhy.reshape(m, HY_WIDTH), pr, i, *out_params,
                       g_final[None], final=(i == depth - 1))
    return xr.reshape(batch, seq, D_MODEL)
```
